```python
import math
import jax, jax.numpy as jnp
from jax import lax
import numpy as np

D_MODEL = 1024
BATCH = 4
SEQ = 8192
DEPTH = 2
DEC_BATCH = 16
DEC_SEQ = 32
PAST_LEN = 2048

CHUNK = 64
N_MIXERS = 2
N_GDN_LAYERS = (DEPTH + 1) // 2
N_MLSTM_LAYERS = DEPTH // 2

GDN_HEADS = 8
GDN_DK = 128
GDN_DV = 128
GDN_KW = GDN_HEADS * GDN_DK
GDN_VW = GDN_HEADS * GDN_DV
GDN_QKV = 2 * GDN_KW + GDN_VW
GDN_IN = GDN_QKV + GDN_VW + 2 * GDN_HEADS
CONV_W = 4
DT_MIN = 1e-3
DT_MAX = 0.1

ML_HEADS = 8
ML_DK = 64
ML_DV = 128
ML_KW = ML_HEADS * ML_DK
ML_VW = ML_HEADS * ML_DV
ML_IN = 2 * ML_KW + 2 * ML_VW + 2 * ML_HEADS

FFN_HIDDEN = ((-(-8 * D_MODEL // 3)) + 255) // 256 * 256

RMS_EPS = 1e-6
L2_EPS = 1e-6
NEG = -1e30

kernel_name = "hybrid_gdn_mlstm_streaming_step"


def rmsnorm(x, g):
    xf = x.astype(jnp.float32)
    y = xf * lax.rsqrt(jnp.mean(xf * xf, -1, keepdims=True) + RMS_EPS)
    return (y * g.astype(jnp.float32)).astype(x.dtype)


def _l2norm(t):
    return t * lax.rsqrt(jnp.sum(t * t, -1, keepdims=True) + L2_EPS)


def _to_chunks(t, L):
    B, T, H, d = t.shape
    return t.reshape(B, T // L, L, H, d).transpose(1, 0, 3, 2, 4)


def _gate_chunks(t, L):
    B, T, H = t.shape
    return t.reshape(B, T // L, L, H).transpose(1, 0, 3, 2)


def _from_chunks(t):
    nc, B, H, L, d = t.shape
    return t.transpose(1, 0, 3, 2, 4).reshape(B, nc * L, H, d)


def causal_dwconv(x, buf, w):
    xp = jnp.concatenate([buf.astype(x.dtype), x], axis=1)
    y = lax.conv_general_dilated(xp, w[:, None, :].astype(x.dtype), (1,), 'VALID',
                                 dimension_numbers=('NWC', 'WIO', 'NWC'),
                                 feature_group_count=x.shape[-1])
    return y, xp[:, -(CONV_W - 1):]


def gated_deltanet(h, conv_buf, S0, w_in, conv_w, a_log, dt_bias, g_norm, w_out, L):
    f32 = jnp.float32
    B, T, _ = h.shape
    H = GDN_HEADS
    proj = h @ w_in
    qkv = proj[..., :GDN_QKV]
    z = proj[..., GDN_QKV:GDN_QKV + GDN_VW]
    a = proj[..., GDN_QKV + GDN_VW:GDN_QKV + GDN_VW + H]
    bb = proj[..., GDN_QKV + GDN_VW + H:]
    qkv_c, new_buf = causal_dwconv(qkv, conv_buf, conv_w)
    qkv_c = jax.nn.silu(qkv_c.astype(f32))
    q = _l2norm(qkv_c[..., :GDN_KW].reshape(B, T, H, GDN_DK)) * (GDN_DK ** -0.5)
    k = _l2norm(qkv_c[..., GDN_KW:2 * GDN_KW].reshape(B, T, H, GDN_DK))
    v = qkv_c[..., 2 * GDN_KW:].reshape(B, T, H, GDN_DV)
    beta = jax.nn.sigmoid(bb.astype(f32))
    g = -jnp.exp(a_log.astype(f32)) * jax.nn.softplus(a.astype(f32) + dt_bias.astype(f32))

    qc, kc, vc = _to_chunks(q, L), _to_chunks(k, L), _to_chunks(v, L)
    bc, gc = _gate_chunks(beta, L), _gate_chunks(g, L)
    decay = jnp.cumsum(gc, axis=-1)
    causal = jnp.tril(jnp.ones((L, L), bool))
    strict = jnp.tril(jnp.ones((L, L), bool), -1)
    diff = decay[..., :, None] - decay[..., None, :]
    gamma = jnp.where(causal, jnp.exp(jnp.where(causal, diff, 0.0)), 0.0)
    kk = jnp.einsum('cbhik,cbhjk->cbhij', kc, kc)
    A = jnp.where(strict, bc[..., :, None] * kk * gamma, 0.0)
    eye = jnp.eye(L, dtype=f32)
    Tm = lax.linalg.triangular_solve(eye + A, jnp.broadcast_to(eye, A.shape),
                                     left_side=True, lower=True, unit_diagonal=True)
    u = jnp.einsum('cbhij,cbhjv->cbhiv', Tm, vc * bc[..., None])
    w = jnp.einsum('cbhij,cbhjk->cbhik', Tm, kc * (bc * jnp.exp(decay))[..., None])
    attn = jnp.einsum('cbhik,cbhjk->cbhij', qc, kc) * gamma
    q_dec = qc * jnp.exp(decay)[..., None]
    k_tail = kc * jnp.exp(decay[..., -1:] - decay)[..., None]
    d_last = jnp.exp(decay[..., -1])

    def step(S, xs):
        u_c, w_c, attn_c, qd_c, kt_c, dl_c = xs
        v_new = u_c - jnp.einsum('bhlk,bhkv->bhlv', w_c, S)
        o = jnp.einsum('bhlk,bhkv->bhlv', qd_c, S) + jnp.einsum('bhij,bhjv->bhiv', attn_c, v_new)
        S = S * dl_c[..., None, None] + jnp.einsum('bhlk,bhlv->bhkv', kt_c, v_new)
        return S, o

    S_fin, o = lax.scan(step, S0.astype(f32), (u, w, attn, q_dec, k_tail, d_last))
    o = _from_chunks(o)
    o = o * lax.rsqrt(jnp.mean(o * o, -1, keepdims=True) + RMS_EPS) * g_norm.astype(f32)
    o = o * jax.nn.silu(z.astype(f32).reshape(B, T, H, GDN_DV))
    y = o.reshape(B, T, GDN_VW).astype(h.dtype) @ w_out
    return y, new_buf, S_fin


def mlstm(h, C0, n0, m0, w_in, b_i, b_f, g_norm, w_out, L):
    f32 = jnp.float32
    B, T, _ = h.shape
    H = ML_HEADS
    proj = h @ w_in
    q = proj[..., :ML_KW].astype(f32).reshape(B, T, H, ML_DK)
    k = proj[..., ML_KW:2 * ML_KW].astype(f32).reshape(B, T, H, ML_DK) * (ML_DK ** -0.5)
    v = proj[..., 2 * ML_KW:2 * ML_KW + ML_VW].astype(f32).reshape(B, T, H, ML_DV)
    o_pre = proj[..., 2 * ML_KW + ML_VW:2 * ML_KW + 2 * ML_VW]
    i_pre = proj[..., 2 * ML_KW + 2 * ML_VW:2 * ML_KW + 2 * ML_VW + H]
    f_pre = proj[..., 2 * ML_KW + 2 * ML_VW + H:]
    ig = i_pre.astype(f32) + b_i.astype(f32)
    lf = jax.nn.log_sigmoid(f_pre.astype(f32) + b_f.astype(f32))

    qc, kc, vc = _to_chunks(q, L), _to_chunks(k, L), _to_chunks(v, L)
    igc, lfc = _gate_chunks(ig, L), _gate_chunks(lf, L)
    b = jnp.cumsum(lfc, axis=-1)
    causal = jnp.tril(jnp.ones((L, L), bool))
    log_d = jnp.where(causal, b[..., :, None] - b[..., None, :] + igc[..., None, :], NEG)
    m_intra = jnp.max(log_d, axis=-1)
    d_hat = jnp.exp(log_d - m_intra[..., None])
    s_hat = jnp.einsum('cbhik,cbhjk->cbhij', qc, kc) * d_hat
    num_hat = jnp.einsum('cbhij,cbhjv->cbhiv', s_hat, vc)
    qn_hat = jnp.sum(s_hat, axis=-1)
    log_w = b[..., -1:] - b + igc
    m_w = jnp.max(log_w, axis=-1)
    w_hat = jnp.exp(log_w - m_w[..., None])
    ckv_hat = jnp.einsum('cbhl,cbhlk,cbhlv->cbhkv', w_hat, kc, vc)
    n_hat = jnp.einsum('cbhl,cbhlk->cbhk', w_hat, kc)
    b_last = b[..., -1]

    def step(carry, xs):
        C, n, m = carry
        q_c, b_c, mi_c, num_c, qn_c, bl_c, mw_c, ckv_c, nh_c = xs
        log_inter = b_c + m[..., None]
        m_t = jnp.maximum(log_inter, mi_c)
        s_inter = jnp.exp(log_inter - m_t)
        s_intra = jnp.exp(mi_c - m_t)
        num = s_inter[..., None] * jnp.einsum('bhlk,bhkv->bhlv', q_c, C) + s_intra[..., None] * num_c
        qn = s_inter * jnp.einsum('bhlk,bhk->bhl', q_c, n) + s_intra * qn_c
        hcell = num / jnp.maximum(jnp.abs(qn), jnp.exp(-m_t))[..., None]
        m_new = jnp.maximum(bl_c + m, mw_c)
        a_old = jnp.exp(bl_c + m - m_new)
        a_new = jnp.exp(mw_c - m_new)
        C = a_old[..., None, None] * C + a_new[..., None, None] * ckv_c
        n = a_old[..., None] * n + a_new[..., None] * nh_c
        return (C, n, m_new), hcell

    (C_f, n_f, m_f), hc = lax.scan(step, (C0.astype(f32), n0.astype(f32), m0.astype(f32)),
                                   (qc, b, m_intra, num_hat, qn_hat, b_last, m_w, ckv_hat, n_hat))
    hc = _from_chunks(hc)
    hc = hc * lax.rsqrt(jnp.mean(hc * hc, -1, keepdims=True) + RMS_EPS)
    hc = hc.reshape(B, T, ML_VW) * g_norm.astype(f32) * jax.nn.sigmoid(o_pre.astype(f32))
    y = hc.astype(h.dtype) @ w_out
    return y, C_f, n_f, m_f


def swiglu(h, w_gu, w_down):
    gu = h @ w_gu
    return (jax.nn.silu(gu[..., :FFN_HIDDEN]) * gu[..., FFN_HIDDEN:]) @ w_down


def _trunk(x, conv0, S0, C0, n0, m0, L, norm_mix, gdn_w_in, gdn_conv_w, gdn_a_log, gdn_dt_bias,
           gdn_norm, gdn_w_out, ml_w_in, ml_b_i, ml_b_f, ml_norm, ml_w_out, norm_ffn, ffn_w_gu,
           ffn_w_down, norm_final):
    convs, Ss, Cs, ns, ms = [], [], [], [], []
    for i in range(DEPTH):
        j = i // N_MIXERS
        hn = rmsnorm(x, norm_mix[i])
        if i % N_MIXERS == 0:
            y, cb, S = gated_deltanet(hn, conv0[j], S0[j], gdn_w_in[j], gdn_conv_w[j], gdn_a_log[j],
                                      gdn_dt_bias[j], gdn_norm[j], gdn_w_out[j], L)
            convs.append(cb)
            Ss.append(S)
        else:
            y, C, n, m = mlstm(hn, C0[j], n0[j], m0[j], ml_w_in[j], ml_b_i[j], ml_b_f[j],
                               ml_norm[j], ml_w_out[j], L)
            Cs.append(C)
            ns.append(n)
            ms.append(m)
        x = x + y
        x = x + swiglu(rmsnorm(x, norm_ffn[i]), ffn_w_gu[i], ffn_w_down[i])
    return (rmsnorm(x, norm_final), jnp.stack(convs), jnp.stack(Ss), jnp.stack(Cs),
            jnp.stack(ns), jnp.stack(ms))


def setup_inputs(seed: int = 0) -> dict:
    key = jax.random.key(seed)
    ks = jax.random.split(key, 24)
    f32 = jnp.float32
    NG, NM = N_GDN_LAYERS, N_MLSTM_LAYERS

    def nrm(k, shape, s):
        return s * jax.random.normal(k, shape, f32)

    dt = jnp.exp(jax.random.uniform(ks[10], (NG, GDN_HEADS), f32, math.log(DT_MIN), math.log(DT_MAX)))
    return {
        "x_prompt": nrm(ks[0], (BATCH, SEQ, D_MODEL), 1.0),
        "x_sample": nrm(ks[1], (DEC_BATCH, DEC_SEQ, D_MODEL), 1.0),
        "state_gdn_conv": nrm(ks[2], (NG, DEC_BATCH, CONV_W - 1, GDN_QKV), 1.0),
        "state_gdn_S": nrm(ks[3], (NG, DEC_BATCH, GDN_HEADS, GDN_DK, GDN_DV), 0.1),
        "state_mlstm_C": nrm(ks[4], (NM, DEC_BATCH, ML_HEADS, ML_DK, ML_DV), 0.3),
        "state_mlstm_n": nrm(ks[5], (NM, DEC_BATCH, ML_HEADS, ML_DK), 0.3),
        "state_mlstm_m": nrm(ks[6], (NM, DEC_BATCH, ML_HEADS), 1.0),
        "norm_mix": 1.0 + nrm(ks[7], (DEPTH, D_MODEL), 0.01),
        "gdn_w_in": nrm(ks[8], (NG, D_MODEL, GDN_IN), D_MODEL ** -0.5),
        "gdn_conv_w": nrm(ks[9], (NG, CONV_W, GDN_QKV), CONV_W ** -0.5),
        "gdn_a_log": jnp.log(jax.random.uniform(ks[11], (NG, GDN_HEADS), f32, 1.0, 16.0)),
        "gdn_dt_bias": dt + jnp.log(-jnp.expm1(-dt)),
        "gdn_norm": 1.0 + nrm(ks[12], (NG, GDN_DV), 0.01),
        "gdn_w_out": nrm(ks[13], (NG, GDN_VW, D_MODEL), GDN_VW ** -0.5),
        "ml_w_in": nrm(ks[14], (NM, D_MODEL, ML_IN), D_MODEL ** -0.5),
        "ml_b_i": nrm(ks[15], (NM, ML_HEADS), 0.1),
        "ml_b_f": 3.0 + nrm(ks[16], (NM, ML_HEADS), 0.5),
        "ml_norm": 1.0 + nrm(ks[17], (NM, ML_VW), 0.01),
        "ml_w_out": nrm(ks[18], (NM, ML_VW, D_MODEL), ML_VW ** -0.5),
        "norm_ffn": 1.0 + nrm(ks[19], (DEPTH, D_MODEL), 0.01),
        "ffn_w_gu": nrm(ks[20], (DEPTH, D_MODEL, 2 * FFN_HIDDEN), D_MODEL ** -0.5),
        "ffn_w_down": nrm(ks[21], (DEPTH, FFN_HIDDEN, D_MODEL), FFN_HIDDEN ** -0.5),
        "norm_final": 1.0 + nrm(ks[22], (D_MODEL,), 0.01),
    }


def reference(x_prompt, x_sample, state_gdn_conv, state_gdn_S, state_mlstm_C, state_mlstm_n,
              state_mlstm_m, norm_mix, gdn_w_in, gdn_conv_w, gdn_a_log, gdn_dt_bias, gdn_norm,
              gdn_w_out, ml_w_in, ml_b_i, ml_b_f, ml_norm, ml_w_out, norm_ffn, ffn_w_gu,
              ffn_w_down, norm_final):
    f32 = jnp.float32
    Bp = x_prompt.shape[0]
    NG, NM = N_GDN_LAYERS, N_MLSTM_LAYERS
    p_conv0 = jnp.zeros((NG, Bp, CONV_W - 1, GDN_QKV), x_prompt.dtype)
    p_S0 = jnp.zeros((NG, Bp, GDN_HEADS, GDN_DK, GDN_DV), f32)
    p_C0 = jnp.zeros((NM, Bp, ML_HEADS, ML_DK, ML_DV), f32)
    p_n0 = jnp.zeros((NM, Bp, ML_HEADS, ML_DK), f32)
    p_m0 = jnp.zeros((NM, Bp, ML_HEADS), f32)
    y_prompt, p_conv, p_S, p_C, p_n, p_m = _trunk(
        x_prompt, p_conv0, p_S0, p_C0, p_n0, p_m0, CHUNK, norm_mix, gdn_w_in, gdn_conv_w,
        gdn_a_log, gdn_dt_bias, gdn_norm, gdn_w_out, ml_w_in, ml_b_i, ml_b_f, ml_norm, ml_w_out,
        norm_ffn, ffn_w_gu, ffn_w_down, norm_final)
    y_sample, s_conv, s_S, s_C, s_n, s_m = _trunk(
        x_sample, state_gdn_conv, state_gdn_S, state_mlstm_C, state_mlstm_n, state_mlstm_m,
        x_sample.shape[1], norm_mix, gdn_w_in, gdn_conv_w, gdn_a_log, gdn_dt_bias, gdn_norm,
        gdn_w_out, ml_w_in, ml_b_i, ml_b_f, ml_norm, ml_w_out, norm_ffn, ffn_w_gu, ffn_w_down,
        norm_final)
    return (y_prompt, y_sample, p_conv, p_S, p_C, p_n, p_m, s_conv, s_S, s_C, s_n, s_m)
```

```python
import functools

import jax
import jax.numpy as jnp
from jax import lax
from jax.experimental import pallas as pl
from jax.experimental.pallas import tpu as pltpu

F32 = jnp.float32
BF16 = jnp.bfloat16

D_MODEL = 1024
PROMPT_CHUNK = 64
GDN_HEADS = 8
GDN_DK = 128
GDN_DV = 128
GDN_KW = GDN_HEADS * GDN_DK
GDN_VW = GDN_HEADS * GDN_DV
GDN_QKV = 2 * GDN_KW + GDN_VW
CONV_W = 4
ML_HEADS = 8
ML_DK = 64
ML_DV = 128
ML_KW = ML_HEADS * ML_DK
ML_VW = ML_HEADS * ML_DV
FFN_HIDDEN = 2816
RMS_EPS = 1e-6
L2_EPS = 1e-6
NEG = -1e30

LANES = 128
SUBLANES = 8
TOKEN_TILE = 512
FFN_CHUNK = 256
VMEM_LIMIT = 56 * 1024 * 1024

NN = ((1,), (0,))
NT = ((1,), (1,))
TN = ((0,), (0,))


def _dot(a, b, dims=NN):
    return lax.dot_general(a.astype(BF16), b.astype(BF16), (dims, ((), ())),
                           preferred_element_type=F32)


def _dot_sel(sel, x, dims=NN):
    hi = x.astype(BF16)
    r1 = x - hi.astype(F32)
    mid = r1.astype(BF16)
    lo = (r1 - mid.astype(F32)).astype(BF16)
    d = lambda p: lax.dot_general(sel, p, (dims, ((), ())), preferred_element_type=F32)
    return d(hi) + d(mid) + d(lo)


def _sigmoid(x):
    return 1.0 / (1.0 + jnp.exp(-x))


def _softplus(x):
    return jnp.maximum(x, 0.0) + jnp.log1p(jnp.exp(-jnp.abs(x)))


def _rms(x, g):
    return x * lax.rsqrt(jnp.mean(x * x, axis=-1, keepdims=True) + RMS_EPS) * g


def _masks(L):
    ri = lax.broadcasted_iota(jnp.int32, (L, L), 0)
    ci = lax.broadcasted_iota(jnp.int32, (L, L), 1)
    return ri >= ci, ri > ci


def _bcast_col(x, c, width):
    return jnp.broadcast_to(x[:, c:c + 1], (x.shape[0], width))


def _norm_proj_kernel(x_ref, g_ref, *refs, n_out):
    w_refs, o_refs = refs[:n_out], refs[n_out:]
    h = _rms(x_ref[...], g_ref[...]).astype(BF16)
    for w_ref, o_ref in zip(w_refs, o_refs):
        o_ref[...] = jnp.dot(h, w_ref[...], preferred_element_type=F32).astype(o_ref.dtype)


def _norm_proj(x, g, weights):
    n, d = x.shape
    tm = min(TOKEN_TILE, n)
    assert n % tm == 0
    in_specs = [pl.BlockSpec((tm, d), lambda i: (i, 0)),
                pl.BlockSpec((1, d), lambda i: (0, 0))]
    in_specs += [pl.BlockSpec(w.shape, lambda i: (0, 0)) for w in weights]
    out_specs = [pl.BlockSpec((tm, w.shape[1]), lambda i: (i, 0)) for w in weights]
    out_shape = [jax.ShapeDtypeStruct((n, w.shape[1]), F32) for w in weights]
    return pl.pallas_call(
        functools.partial(_norm_proj_kernel, n_out=len(weights)),
        grid=(n // tm,),
        in_specs=in_specs, out_specs=out_specs, out_shape=out_shape,
        compiler_params=pltpu.CompilerParams(
            dimension_semantics=("arbitrary",), vmem_limit_bytes=VMEM_LIMIT),
        name="norm_proj",
    )(x, g, *weights)


def _unit_lower_inverse(a, L):
    ri = lax.broadcasted_iota(jnp.int32, (L, L), 0)
    ci = lax.broadcasted_iota(jnp.int32, (L, L), 1)
    npow = -a
    p = jnp.where(ri == ci, 1.0, 0.0) + npow
    span = 2
    while span < L:
        npow = _dot(npow, npow)
        p = p + _dot(p, npow)
        span *= 2
    return p


def _gdn_kernel(qkv_ref, z_ref, gates_ref, prev_ref, s0_ref, convw_ref, alog_ref, dtb_ref, gn_ref,
                o_ref, sout_ref, tail_scr, s_scr, *, L):
    H, DK, DV, KW = GDN_HEADS, GDN_DK, GDN_DV, GDN_KW
    t = pl.program_id(1)

    @pl.when(t == 0)
    def _():
        tail_scr[...] = prev_ref[...]
        s_scr[...] = s0_ref[...]

    x = qkv_ref[...]
    xp = jnp.concatenate([tail_scr[...], x], axis=0)
    cw = convw_ref[...]
    base = SUBLANES - (CONV_W - 1)
    y = xp[base:base + L] * cw[0:1]
    for j in range(1, CONV_W):
        y = y + xp[base + j:base + j + L] * cw[j:j + 1]
    tail_scr[...] = x[L - SUBLANES:L]
    y = y * _sigmoid(y)

    gt = gates_ref[...]
    g_all = -jnp.exp(alog_ref[...]) * _softplus(gt + dtb_ref[...])
    beta_all = _sigmoid(gt)
    causal, strict = _masks(L)
    tril = jnp.where(causal, 1.0, 0.0).astype(BF16)
    dec = _dot_sel(tril, g_all)
    dec_t = dec.T
    gn = gn_ref[...]

    for h in range(H):
        qh = y[:, h * DK:(h + 1) * DK]
        kh = y[:, KW + h * DK:KW + (h + 1) * DK]
        vh = y[:, 2 * KW + h * DV:2 * KW + (h + 1) * DV]
        qh = qh * lax.rsqrt(jnp.sum(qh * qh, axis=-1, keepdims=True) + L2_EPS) * (DK ** -0.5)
        kh = kh * lax.rsqrt(jnp.sum(kh * kh, axis=-1, keepdims=True) + L2_EPS)

        dcol = _bcast_col(dec, h, DK)
        bcol = _bcast_col(beta_all, H + h, DK)
        drow = dec_t[h:h + 1, :]
        diff = dcol[:, :L] - drow
        gamma = jnp.where(causal, jnp.exp(jnp.where(causal, diff, 0.0)), 0.0)
        kk = _dot(kh, kh, NT)
        a = jnp.where(strict, bcol[:, :L] * kk * gamma, 0.0)
        tm = _unit_lower_inverse(a, L)

        edec = jnp.exp(dcol)
        u = _dot(tm, vh * bcol)
        w = _dot(tm, kh * (bcol * edec))
        attn = _dot(qh, kh, NT) * gamma
        dlast = dcol[L - 1:L, :]
        q_dec = qh * edec
        k_tail = kh * jnp.exp(dlast - dcol)

        s = s_scr[h]
        v_new = u - _dot(w, s)
        o = _dot(q_dec, s) + _dot(attn, v_new)
        s_scr[h] = s * jnp.exp(dlast) + _dot(k_tail, v_new, TN)

        o = o * lax.rsqrt(jnp.mean(o * o, axis=-1, keepdims=True) + RMS_EPS) * gn
        zh = z_ref[:, h * DV:(h + 1) * DV]
        o_ref[:, h * DV:(h + 1) * DV] = (o * (zh * _sigmoid(zh))).astype(o_ref.dtype)

    @pl.when(t == pl.num_programs(1) - 1)
    def _():
        sout_ref[...] = s_scr[...]


def _gdn_recurrence(qkv, z, gates, prev8, s0, conv_w, alog_row, dtb_row, gnorm_row, B, T, L):
    nt = T // L
    H, DK, DV = GDN_HEADS, GDN_DK, GDN_DV
    row = lambda b, t: (b * nt + t, 0)
    const2 = lambda b, t: (0, 0)
    return pl.pallas_call(
        functools.partial(_gdn_kernel, L=L),
        grid=(B, nt),
        in_specs=[
            pl.BlockSpec((L, GDN_QKV), row),
            pl.BlockSpec((L, GDN_VW), row),
            pl.BlockSpec((L, LANES), row),
            pl.BlockSpec((None, SUBLANES, GDN_QKV), lambda b, t: (b, 0, 0)),
            pl.BlockSpec((None, H, DK, DV), lambda b, t: (b, 0, 0, 0)),
            pl.BlockSpec((CONV_W, GDN_QKV), const2),
            pl.BlockSpec((1, LANES), const2),
            pl.BlockSpec((1, LANES), const2),
            pl.BlockSpec((1, DV), const2),
        ],
        out_specs=[
            pl.BlockSpec((L, GDN_VW), row),
            pl.BlockSpec((None, H, DK, DV), lambda b, t: (b, 0, 0, 0)),
        ],
        out_shape=[
            jax.ShapeDtypeStruct((B * T, GDN_VW), F32),
            jax.ShapeDtypeStruct((B, H, DK, DV), F32),
        ],
        scratch_shapes=[
            pltpu.VMEM((SUBLANES, GDN_QKV), F32),
            pltpu.VMEM((H, DK, DV), F32),
        ],
        compiler_params=pltpu.CompilerParams(
            dimension_semantics=("arbitrary", "arbitrary"), vmem_limit_bytes=VMEM_LIMIT),
        name="gdn_recurrence",
    )(qkv, z, gates, prev8, s0, conv_w, alog_row, dtb_row, gnorm_row)


def _mlstm_kernel(q_ref, k_ref, v_ref, op_ref, gates_ref, c0_ref, n0_ref, m0_ref, bias_ref, gn_ref,
                  o_ref, cout_ref, nout_ref, mout_ref, c_scr, n_scr, m_scr, *, L):
    H, DK, DV = ML_HEADS, ML_DK, ML_DV
    t = pl.program_id(1)

    @pl.when(t == 0)
    def _():
        c_scr[...] = c0_ref[...]
        n_scr[...] = n0_ref[...]
        m_scr[...] = m0_ref[...]

    pre = gates_ref[...] + bias_ref[...]
    lf_all = -_softplus(-pre)
    causal, _ = _masks(L)
    tril = jnp.where(causal, 1.0, 0.0).astype(BF16)
    bcum = _dot_sel(tril, lf_all)
    bcum_t = bcum.T
    pre_t = pre.T

    for h in range(H):
        qh = q_ref[:, h * DK:(h + 1) * DK]
        kh = k_ref[:, h * DK:(h + 1) * DK] * (DK ** -0.5)
        vh = v_ref[:, h * DV:(h + 1) * DV]

        bcol = _bcast_col(bcum, H + h, DV)
        icol = _bcast_col(pre, h, DV)
        brow = bcum_t[H + h:H + h + 1, :]
        irow = pre_t[h:h + 1, :]

        log_d = jnp.where(causal, bcol[:, :L] - brow + irow, NEG)
        m_intra = jnp.max(log_d, axis=-1, keepdims=True)
        d_hat = jnp.exp(log_d - m_intra)
        s_hat = _dot(qh, kh, NT) * d_hat
        num_hat = _dot(s_hat, vh)
        qn_hat = jnp.sum(s_hat, axis=-1, keepdims=True)

        blast = bcol[L - 1:L, :]
        log_w = blast - bcol + icol
        m_w = jnp.max(log_w, axis=0, keepdims=True)
        w_hat = jnp.exp(log_w - m_w)
        kw = kh * w_hat[:, :DK]
        ckv_hat = _dot(kw, vh, TN)
        n_hat = jnp.sum(kw, axis=0, keepdims=True)

        c = c_scr[h]
        n = n_scr[h:h + 1, :]
        m = m_scr[h:h + 1, :]
        log_inter = bcol + m
        m_t = jnp.maximum(log_inter, m_intra)
        s_inter = jnp.exp(log_inter - m_t)
        s_intra = jnp.exp(m_intra - m_t)
        num = s_inter * _dot(qh, c) + s_intra * num_hat
        qn = s_inter * jnp.sum(qh * n, axis=-1, keepdims=True) + s_intra * qn_hat
        hcell = num / jnp.maximum(jnp.abs(qn), jnp.exp(-m_t))

        m_new = jnp.maximum(blast + m, m_w)
        a_old = jnp.exp(blast + m - m_new)
        a_new = jnp.exp(m_w - m_new)
        c_scr[h] = a_old * c + a_new * ckv_hat
        n_scr[h:h + 1, :] = a_old[:, :DK] * n + a_new[:, :DK] * n_hat
        m_scr[h:h + 1, :] = m_new

        hc = hcell * lax.rsqrt(jnp.mean(hcell * hcell, axis=-1, keepdims=True) + RMS_EPS)
        oh = op_ref[:, h * DV:(h + 1) * DV]
        o_ref[:, h * DV:(h + 1) * DV] = (
            hc * gn_ref[:, h * DV:(h + 1) * DV] * _sigmoid(oh)).astype(o_ref.dtype)

    @pl.when(t == pl.num_programs(1) - 1)
    def _():
        cout_ref[...] = c_scr[...]
        nout_ref[...] = n_scr[...]
        mout_ref[...] = m_scr[...]


def _mlstm_recurrence(q, k, v, o_pre, gates, c0, n0, m0x, bias_row, gnorm_row, B, T, L):
    nt = T // L
    H, DK, DV = ML_HEADS, ML_DK, ML_DV
    row = lambda b, t: (b * nt + t, 0)
    const2 = lambda b, t: (0, 0)
    st4 = lambda b, t: (b, 0, 0, 0)
    st3 = lambda b, t: (b, 0, 0)
    return pl.pallas_call(
        functools.partial(_mlstm_kernel, L=L),
        grid=(B, nt),
        in_specs=[
            pl.BlockSpec((L, ML_KW), row),
            pl.BlockSpec((L, ML_KW), row),
            pl.BlockSpec((L, ML_VW), row),
            pl.BlockSpec((L, ML_VW), row),
            pl.BlockSpec((L, LANES), row),
            pl.BlockSpec((None, H, DK, DV), st4),
            pl.BlockSpec((None, H, DK), st3),
            pl.BlockSpec((None, H, LANES), st3),
            pl.BlockSpec((1, LANES), const2),
            pl.BlockSpec((1, ML_VW), const2),
        ],
        out_specs=[
            pl.BlockSpec((L, ML_VW), row),
            pl.BlockSpec((None, H, DK, DV), st4),
            pl.BlockSpec((None, H, DK), st3),
            pl.BlockSpec((None, H, LANES), st3),
        ],
        out_shape=[
            jax.ShapeDtypeStruct((B * T, ML_VW), F32),
            jax.ShapeDtypeStruct((B, H, DK, DV), F32),
            jax.ShapeDtypeStruct((B, H, DK), F32),
            jax.ShapeDtypeStruct((B, H, LANES), F32),
        ],
        scratch_shapes=[
            pltpu.VMEM((H, DK, DV), F32),
            pltpu.VMEM((H, DK), F32),
            pltpu.VMEM((H, LANES), F32),
        ],
        compiler_params=pltpu.CompilerParams(
            dimension_semantics=("arbitrary", "arbitrary"), vmem_limit_bytes=VMEM_LIMIT),
        name="mlstm_recurrence",
    )(q, k, v, o_pre, gates, c0, n0, m0x, bias_row, gnorm_row)


def _out_ffn_kernel(x_ref, o_ref, wo_ref, g_ref, wgu_ref, wd_ref, gfin_ref, y_ref, *, final):
    x1 = x_ref[...] + jnp.dot(o_ref[...].astype(BF16), wo_ref[...], preferred_element_type=F32)
    hn = _rms(x1, g_ref[...]).astype(BF16)
    acc = x1
    for c in range(FFN_HIDDEN // FFN_CHUNK):
        lo = c * FFN_CHUNK
        gate = jnp.dot(hn, wgu_ref[:, lo:lo + FFN_CHUNK], preferred_element_type=F32)
        up = jnp.dot(hn, wgu_ref[:, FFN_HIDDEN + lo:FFN_HIDDEN + lo + FFN_CHUNK],
                     preferred_element_type=F32)
        act = (gate * _sigmoid(gate) * up).astype(BF16)
        acc = acc + jnp.dot(act, wd_ref[lo:lo + FFN_CHUNK, :], preferred_element_type=F32)
    if final:
        acc = _rms(acc, gfin_ref[...])
    y_ref[...] = acc


def _out_ffn(x, o, w_out, g, w_gu, w_down, g_final, final):
    n, d = x.shape
    tm = min(TOKEN_TILE, n)
    assert n % tm == 0 and FFN_HIDDEN % FFN_CHUNK == 0
    tile = lambda i: (i, 0)
    const = lambda i: (0, 0)
    return pl.pallas_call(
        functools.partial(_out_ffn_kernel, final=final),
        grid=(n // tm,),
        in_specs=[
            pl.BlockSpec((tm, d), tile),
            pl.BlockSpec((tm, o.shape[1]), tile),
            pl.BlockSpec(w_out.shape, const),
            pl.BlockSpec((1, d), const),
            pl.BlockSpec(w_gu.shape, const),
            pl.BlockSpec(w_down.shape, const),
            pl.BlockSpec((1, d), const),
        ],
        out_specs=pl.BlockSpec((tm, d), tile),
        out_shape=jax.ShapeDtypeStruct((n, d), F32),
        compiler_params=pltpu.CompilerParams(
            dimension_semantics=("arbitrary",), vmem_limit_bytes=VMEM_LIMIT),
        name="out_ffn",
    )(x, o, w_out, g, w_gu, w_down, g_final)


def _pad_lanes(row, offset=0):
    out = jnp.zeros((1, LANES), F32)
    return lax.dynamic_update_slice(out, row.astype(F32)[None, :], (0, offset))


def _prep_weights(norm_mix, gdn_w_in, gdn_conv_w, gdn_a_log, gdn_dt_bias, gdn_norm, gdn_w_out,
                  ml_w_in, ml_b_i, ml_b_f, ml_norm, ml_w_out, norm_ffn, ffn_w_gu, ffn_w_down,
                  norm_final):
    H = GDN_HEADS
    w0 = gdn_w_in[0]
    gate_pad = jnp.zeros((D_MODEL, LANES - 2 * H), F32)
    w1 = ml_w_in[0]
    p = {
        "g_mix0": norm_mix[0][None, :], "g_mix1": norm_mix[1][None, :],
        "g_ffn0": norm_ffn[0][None, :], "g_ffn1": norm_ffn[1][None, :],
        "g_final": norm_final[None, :],
        "gdn_w_qkv": w0[:, :GDN_QKV].astype(BF16),
        "gdn_w_z": w0[:, GDN_QKV:GDN_QKV + GDN_VW].astype(BF16),
        "gdn_w_gates": jnp.concatenate([w0[:, GDN_QKV + GDN_VW:], gate_pad], axis=1).astype(BF16),
        "gdn_conv_w": gdn_conv_w[0],
        "gdn_alog": _pad_lanes(gdn_a_log[0]), "gdn_dtb": _pad_lanes(gdn_dt_bias[0]),
        "gdn_norm": gdn_norm[0][None, :],
        "gdn_w_out": gdn_w_out[0].astype(BF16),
        "ml_w_q": w1[:, :ML_KW].astype(BF16),
        "ml_w_k": w1[:, ML_KW:2 * ML_KW].astype(BF16),
        "ml_w_v": w1[:, 2 * ML_KW:2 * ML_KW + ML_VW].astype(BF16),
        "ml_w_o": w1[:, 2 * ML_KW + ML_VW:2 * ML_KW + 2 * ML_VW].astype(BF16),
        "ml_w_gates": jnp.concatenate([w1[:, 2 * ML_KW + 2 * ML_VW:], gate_pad], axis=1).astype(BF16),
        "ml_bias": _pad_lanes(jnp.concatenate([ml_b_i[0], ml_b_f[0]])),
        "ml_norm": ml_norm[0][None, :],
        "ml_w_out": ml_w_out[0].astype(BF16),
        "ffn_w_gu0": ffn_w_gu[0].astype(BF16), "ffn_w_gu1": ffn_w_gu[1].astype(BF16),
        "ffn_w_down0": ffn_w_down[0].astype(BF16), "ffn_w_down1": ffn_w_down[1].astype(BF16),
    }
    return p


def _trunk(x, conv0, s0, c0, n0, m0, L, p):
    B, T, D = x.shape
    x2 = x.reshape(B * T, D)

    qkv, z, gates = _norm_proj(x2, p["g_mix0"], [p["gdn_w_qkv"], p["gdn_w_z"], p["gdn_w_gates"]])
    prev8 = jnp.concatenate(
        [jnp.zeros((B, SUBLANES - (CONV_W - 1), GDN_QKV), F32), conv0.astype(F32)], axis=1)
    o, s_fin = _gdn_recurrence(qkv, z, gates, prev8, s0, p["gdn_conv_w"], p["gdn_alog"],
                               p["gdn_dtb"], p["gdn_norm"], B, T, L)
    conv_fin = qkv.reshape(B, T, GDN_QKV)[:, T - (CONV_W - 1):, :]
    x2 = _out_ffn(x2, o, p["gdn_w_out"], p["g_ffn0"], p["ffn_w_gu0"], p["ffn_w_down0"],
                  p["g_final"], final=False)

    q, k, v, o_pre, gates = _norm_proj(
        x2, p["g_mix1"], [p["ml_w_q"], p["ml_w_k"], p["ml_w_v"], p["ml_w_o"], p["ml_w_gates"]])
    m0x = jnp.broadcast_to(m0[:, :, None], (B, ML_HEADS, LANES))
    o, c_fin, n_fin, m_fin = _mlstm_recurrence(q, k, v, o_pre, gates, c0, n0, m0x, p["ml_bias"],
                                               p["ml_norm"], B, T, L)
    y = _out_ffn(x2, o, p["ml_w_out"], p["g_ffn1"], p["ffn_w_gu1"], p["ffn_w_down1"],
                 p["g_final"], final=True)
    return (y.reshape(B, T, D), conv_fin[None], s_fin[None], c_fin[None], n_fin[None],
            m_fin[:, :, 0][None])


def kernel(x_prompt, x_sample, state_gdn_conv, state_gdn_S, state_mlstm_C, state_mlstm_n,
           state_mlstm_m, norm_mix, gdn_w_in, gdn_conv_w, gdn_a_log, gdn_dt_bias, gdn_norm,
           gdn_w_out, ml_w_in, ml_b_i, ml_b_f, ml_norm, ml_w_out, norm_ffn, ffn_w_gu,
           ffn_w_down, norm_final):
    p = _prep_weights(norm_mix, gdn_w_in, gdn_conv_w, gdn_a_log, gdn_dt_bias, gdn_norm, gdn_w_out,
                      ml_w_in, ml_b_i, ml_b_f, ml_norm, ml_w_out, norm_ffn, ffn_w_gu, ffn_w_down,
                      norm_final)
    Bp = x_prompt.shape[0]
    prompt = _trunk(
        x_prompt,
        jnp.zeros((Bp, CONV_W - 1, GDN_QKV), F32),
        jnp.zeros((Bp, GDN_HEADS, GDN_DK, GDN_DV), F32),
        jnp.zeros((Bp, ML_HEADS, ML_DK, ML_DV), F32),
        jnp.zeros((Bp, ML_HEADS, ML_DK), F32),
        jnp.zeros((Bp, ML_HEADS), F32),
        PROMPT_CHUNK, p)
    sample = _trunk(
        x_sample, state_gdn_conv[0], state_gdn_S[0], state_mlstm_C[0], state_mlstm_n[0],
        state_mlstm_m[0], x_sample.shape[1], p)
    return (prompt[0], sample[0]) + prompt[1:] + sample[1:]
```

```python
import functools

import jax
import jax.numpy as jnp
from jax import lax
from jax.experimental import pallas as pl
from jax.experimental.pallas import tpu as pltpu

F32 = jnp.float32
BF16 = jnp.bfloat16

D_MODEL = 1024
PROMPT_CHUNK = 64
GDN_HEADS = 8
GDN_DK = 128
GDN_DV = 128
GDN_KW = GDN_HEADS * GDN_DK
GDN_VW = GDN_HEADS * GDN_DV
GDN_QKV = 2 * GDN_KW + GDN_VW
CONV_W = 4
ML_HEADS = 8
ML_DK = 64
ML_DV = 128
ML_KW = ML_HEADS * ML_DK
ML_VW = ML_HEADS * ML_DV
FFN_HIDDEN = 2816
RMS_EPS = 1e-6
L2_EPS = 1e-6
NEG = -1e30

LANES = 128
SUBLANES = 8
TOKEN_TILE = 512
FFN_CHUNK = 256
GDN_SEQS_PER_STEP = 2
ML_SEQS_PER_STEP = 4
VMEM_LIMIT = 56 * 1024 * 1024

NN = ((1,), (0,))
NT = ((1,), (1,))
TN = ((0,), (0,))


def _dot(a, b, dims=NN):
    return lax.dot_general(a.astype(BF16), b.astype(BF16), (dims, ((), ())),
                           preferred_element_type=F32)


def _dot_sel(sel, x, dims=NN):
    hi = x.astype(BF16)
    r1 = x - hi.astype(F32)
    mid = r1.astype(BF16)
    lo = (r1 - mid.astype(F32)).astype(BF16)
    d = lambda p: lax.dot_general(sel, p, (dims, ((), ())), preferred_element_type=F32)
    return d(hi) + d(mid) + d(lo)


def _sigmoid(x):
    return 1.0 / (1.0 + jnp.exp(-x))


def _softplus(x):
    return jnp.maximum(x, 0.0) + jnp.log1p(jnp.exp(-jnp.abs(x)))


def _rms(x, g):
    return x * lax.rsqrt(jnp.mean(x * x, axis=-1, keepdims=True) + RMS_EPS) * g


def _masks(L):
    ri = lax.broadcasted_iota(jnp.int32, (L, L), 0)
    ci = lax.broadcasted_iota(jnp.int32, (L, L), 1)
    return ri >= ci, ri > ci


def _bcast_col(x, c, width):
    return jnp.broadcast_to(x[:, c:c + 1], (x.shape[0], width))


def _norm_proj_kernel(x_ref, g_ref, *refs, n_out):
    w_refs, o_refs = refs[:n_out], refs[n_out:]
    h = _rms(x_ref[...], g_ref[...]).astype(BF16)
    for w_ref, o_ref in zip(w_refs, o_refs):
        o_ref[...] = jnp.dot(h, w_ref[...], preferred_element_type=F32).astype(o_ref.dtype)


def _norm_proj(x, g, weights):
    n, d = x.shape
    tm = min(TOKEN_TILE, n)
    assert n % tm == 0
    in_specs = [pl.BlockSpec((tm, d), lambda i: (i, 0)),
                pl.BlockSpec((1, d), lambda i: (0, 0))]
    in_specs += [pl.BlockSpec(w.shape, lambda i: (0, 0)) for w in weights]
    out_specs = [pl.BlockSpec((tm, w.shape[1]), lambda i: (i, 0)) for w in weights]
    out_shape = [jax.ShapeDtypeStruct((n, w.shape[1]), F32) for w in weights]
    return pl.pallas_call(
        functools.partial(_norm_proj_kernel, n_out=len(weights)),
        grid=(n // tm,),
        in_specs=in_specs, out_specs=out_specs, out_shape=out_shape,
        compiler_params=pltpu.CompilerParams(
            dimension_semantics=("arbitrary",), vmem_limit_bytes=VMEM_LIMIT),
        name="norm_proj",
    )(x, g, *weights)


def _unit_lower_inverses(a_list, L):
    ri = lax.broadcasted_iota(jnp.int32, (L, L), 0)
    ci = lax.broadcasted_iota(jnp.int32, (L, L), 1)
    eye = jnp.where(ri == ci, 1.0, 0.0)
    ms = [-a for a in a_list]
    ps = [eye + m for m in ms]
    ms = [_dot(m, m) for m in ms]
    span = 2
    while span < L:
        if 2 * span >= L:
            ps = [p + _dot(p, m) for p, m in zip(ps, ms)]
        else:
            rs = [_dot(jnp.concatenate([p, m], axis=0), m) for p, m in zip(ps, ms)]
            ps = [p + r[:L] for p, r in zip(ps, rs)]
            ms = [r[L:] for r in rs]
        span *= 2
    return ps


def _gdn_kernel(qkv_ref, z_ref, gates_ref, prev_ref, s0_ref, convw_ref, alog_ref, dtb_ref, gn_ref,
                o_ref, sout_ref, tail_scr, s_scr, *, L, BB):
    H, DK, DV, KW = GDN_HEADS, GDN_DK, GDN_DV, GDN_KW
    t = pl.program_id(1)

    @pl.when(t == 0)
    def _():
        tail_scr[...] = prev_ref[...]
        s_scr[...] = s0_ref[...]

    causal, strict = _masks(L)
    tril = jnp.where(causal, 1.0, 0.0).astype(BF16)
    cw = convw_ref[...]
    gn = gn_ref[...]
    base = SUBLANES - (CONV_W - 1)

    y, dec, dec_t, beta_all = [], [], [], []
    for b in range(BB):
        x = qkv_ref[b]
        xp = jnp.concatenate([tail_scr[b], x], axis=0)
        yb = xp[base:base + L] * cw[0:1]
        for j in range(1, CONV_W):
            yb = yb + xp[base + j:base + j + L] * cw[j:j + 1]
        tail_scr[b] = x[L - SUBLANES:L]
        y.append(yb * _sigmoid(yb))
        gt = gates_ref[b]
        g_all = -jnp.exp(alog_ref[...]) * _softplus(gt + dtb_ref[...])
        beta_all.append(_sigmoid(gt))
        dec.append(_dot_sel(tril, g_all))
        dec_t.append(dec[b].T)

    items = [(b, h) for b in range(BB) for h in range(H)]
    idx = range(len(items))
    q, k, v = [], [], []
    for b, h in items:
        qh = y[b][:, h * DK:(h + 1) * DK]
        kh = y[b][:, KW + h * DK:KW + (h + 1) * DK]
        q.append(qh * lax.rsqrt(jnp.sum(qh * qh, axis=-1, keepdims=True) + L2_EPS) * (DK ** -0.5))
        k.append(kh * lax.rsqrt(jnp.sum(kh * kh, axis=-1, keepdims=True) + L2_EPS))
        v.append(y[b][:, 2 * KW + h * DV:2 * KW + (h + 1) * DV])

    dcol = [_bcast_col(dec[b], h, DK) for b, h in items]
    bcol = [_bcast_col(beta_all[b], H + h, DK) for b, h in items]
    gamma = []
    for i, (b, h) in enumerate(items):
        diff = dcol[i][:, :L] - dec_t[b][h:h + 1, :]
        gamma.append(jnp.where(causal, jnp.exp(jnp.where(causal, diff, 0.0)), 0.0))

    qk = [_dot(jnp.concatenate([q[i], k[i]], axis=0), k[i], NT) for i in idx]
    attn = [qk[i][:L] * gamma[i] for i in idx]
    a = [jnp.where(strict, bcol[i][:, :L] * qk[i][L:] * gamma[i], 0.0) for i in idx]
    tm = _unit_lower_inverses(a, L)

    edec = [jnp.exp(dcol[i]) for i in idx]
    uw = [_dot(tm[i], jnp.concatenate([v[i] * bcol[i], k[i] * (bcol[i] * edec[i])], axis=1))
          for i in idx]
    dlast = [dcol[i][L - 1:L, :] for i in idx]
    k_tail = [k[i] * jnp.exp(dlast[i] - dcol[i]) for i in idx]

    s = [s_scr[b, h] for b, h in items]
    wq = [_dot(jnp.concatenate([uw[i][:, DV:], q[i] * edec[i]], axis=0), s[i]) for i in idx]
    v_new = [uw[i][:, :DV] - wq[i][:L] for i in idx]
    o_intra = [_dot(attn[i], v_new[i]) for i in idx]
    ds = [_dot(k_tail[i], v_new[i], TN) for i in idx]
    for i, (b, h) in enumerate(items):
        s_scr[b, h] = s[i] * jnp.exp(dlast[i]) + ds[i]
    o = [wq[i][L:] + o_intra[i] for i in idx]
    ms = [jnp.mean(o[i] * o[i], axis=-1, keepdims=True) for i in idx]
    for i, (b, h) in enumerate(items):
        zh = z_ref[b, :, h * DV:(h + 1) * DV]
        o_ref[b, :, h * DV:(h + 1) * DV] = (
            o[i] * lax.rsqrt(ms[i] + RMS_EPS) * gn * (zh * _sigmoid(zh))).astype(o_ref.dtype)

    @pl.when(t == pl.num_programs(1) - 1)
    def _():
        sout_ref[...] = s_scr[...]


def _seqs_per_step(B, want):
    bb = min(B, want)
    assert B % bb == 0
    return bb


def _gdn_recurrence(qkv, z, gates, prev8, s0, conv_w, alog_row, dtb_row, gnorm_row, B, T, L):
    nt = T // L
    H, DK, DV = GDN_HEADS, GDN_DK, GDN_DV
    BB = _seqs_per_step(B, GDN_SEQS_PER_STEP)
    tok = lambda b, t: (b, t, 0)
    per_seq3 = lambda b, t: (b, 0, 0)
    per_seq4 = lambda b, t: (b, 0, 0, 0)
    const2 = lambda b, t: (0, 0)
    return pl.pallas_call(
        functools.partial(_gdn_kernel, L=L, BB=BB),
        grid=(B // BB, nt),
        in_specs=[
            pl.BlockSpec((BB, L, GDN_QKV), tok),
            pl.BlockSpec((BB, L, GDN_VW), tok),
            pl.BlockSpec((BB, L, LANES), tok),
            pl.BlockSpec((BB, SUBLANES, GDN_QKV), per_seq3),
            pl.BlockSpec((BB, H, DK, DV), per_seq4),
            pl.BlockSpec((CONV_W, GDN_QKV), const2),
            pl.BlockSpec((1, LANES), const2),
            pl.BlockSpec((1, LANES), const2),
            pl.BlockSpec((1, DV), const2),
        ],
        out_specs=[
            pl.BlockSpec((BB, L, GDN_VW), tok),
            pl.BlockSpec((BB, H, DK, DV), per_seq4),
        ],
        out_shape=[
            jax.ShapeDtypeStruct((B, T, GDN_VW), BF16),
            jax.ShapeDtypeStruct((B, H, DK, DV), F32),
        ],
        scratch_shapes=[
            pltpu.VMEM((BB, SUBLANES, GDN_QKV), F32),
            pltpu.VMEM((BB, H, DK, DV), F32),
        ],
        compiler_params=pltpu.CompilerParams(
            dimension_semantics=("arbitrary", "arbitrary"), vmem_limit_bytes=VMEM_LIMIT),
        name="gdn_recurrence",
    )(qkv, z, gates, prev8, s0, conv_w, alog_row, dtb_row, gnorm_row)


def _mlstm_kernel(q_ref, k_ref, v_ref, op_ref, gates_ref, c0_ref, n0_ref, m0_ref, bias_ref, gn_ref,
                  o_ref, cout_ref, nout_ref, mout_ref, c_scr, n_scr, m_scr, *, L, BB):
    H, DK, DV = ML_HEADS, ML_DK, ML_DV
    t = pl.program_id(1)

    @pl.when(t == 0)
    def _():
        c_scr[...] = c0_ref[...]
        n_scr[...] = n0_ref[...]
        m_scr[...] = m0_ref[...]

    causal, _ = _masks(L)
    tril = jnp.where(causal, 1.0, 0.0).astype(BF16)
    pre, bcum, bcum_t, pre_t = [], [], [], []
    for b in range(BB):
        pre.append(gates_ref[b] + bias_ref[...])
        bcum.append(_dot_sel(tril, -_softplus(-pre[b])))
        bcum_t.append(bcum[b].T)
        pre_t.append(pre[b].T)

    items = [(b, h) for b in range(BB) for h in range(H)]
    idx = range(len(items))
    q = [q_ref[b, :, h * DK:(h + 1) * DK] for b, h in items]
    k = [k_ref[b, :, h * DK:(h + 1) * DK] * (DK ** -0.5) for b, h in items]
    v = [v_ref[b, :, h * DV:(h + 1) * DV] for b, h in items]
    c = [c_scr[b, h] for b, h in items]
    qk = [_dot(q[i], k[i], NT) for i in idx]
    qc = [_dot(q[i], c[i]) for i in idx]

    bcol = [_bcast_col(bcum[b], H + h, DV) for b, h in items]
    icol = [_bcast_col(pre[b], h, DV) for b, h in items]
    m_intra, s_hat = [], []
    for i, (b, h) in enumerate(items):
        brow = bcum_t[b][H + h:H + h + 1, :]
        irow = pre_t[b][h:h + 1, :]
        log_d = jnp.where(causal, bcol[i][:, :L] - brow + irow, NEG)
        mi = jnp.max(log_d, axis=-1, keepdims=True)
        m_intra.append(mi)
        s_hat.append(qk[i] * jnp.exp(log_d - mi))
    num_hat = [_dot(s_hat[i], v[i]) for i in idx]

    blast = [bcol[i][L - 1:L, :] for i in idx]
    m_w, kw = [], []
    for i in idx:
        log_w = blast[i] - bcol[i] + icol[i]
        mw = jnp.max(log_w, axis=0, keepdims=True)
        m_w.append(mw)
        kw.append(k[i] * jnp.exp(log_w - mw)[:, :DK])
    ckv_hat = [_dot(kw[i], v[i], TN) for i in idx]

    n = [n_scr[b, h:h + 1, :] for b, h in items]
    m = [m_scr[b, h:h + 1, :] for b, h in items]
    qn_hat = [jnp.sum(s_hat[i], axis=-1, keepdims=True) for i in idx]
    qn_inter = [jnp.sum(q[i] * n[i], axis=-1, keepdims=True) for i in idx]
    hcell = []
    for i in idx:
        log_inter = bcol[i] + m[i]
        m_t = jnp.maximum(log_inter, m_intra[i])
        s_inter = jnp.exp(log_inter - m_t)
        s_intra = jnp.exp(m_intra[i] - m_t)
        num = s_inter * qc[i] + s_intra * num_hat[i]
        qn = s_inter * qn_inter[i] + s_intra * qn_hat[i]
        hcell.append(num / jnp.maximum(jnp.abs(qn), jnp.exp(-m_t)))
    ms = [jnp.mean(hcell[i] * hcell[i], axis=-1, keepdims=True) for i in idx]
    for i, (b, h) in enumerate(items):
        oh = op_ref[b, :, h * DV:(h + 1) * DV]
        o_ref[b, :, h * DV:(h + 1) * DV] = (
            hcell[i] * lax.rsqrt(ms[i] + RMS_EPS) * gn_ref[:, h * DV:(h + 1) * DV]
            * _sigmoid(oh)).astype(o_ref.dtype)

    for i, (b, h) in enumerate(items):
        n_hat = jnp.sum(kw[i], axis=0, keepdims=True)
        m_new = jnp.maximum(blast[i] + m[i], m_w[i])
        a_old = jnp.exp(blast[i] + m[i] - m_new)
        a_new = jnp.exp(m_w[i] - m_new)
        c_scr[b, h] = a_old * c[i] + a_new * ckv_hat[i]
        n_scr[b, h:h + 1, :] = a_old[:, :DK] * n[i] + a_new[:, :DK] * n_hat
        m_scr[b, h:h + 1, :] = m_new

    @pl.when(t == pl.num_programs(1) - 1)
    def _():
        cout_ref[...] = c_scr[...]
        nout_ref[...] = n_scr[...]
        mout_ref[...] = m_scr[...]


def _mlstm_recurrence(q, k, v, o_pre, gates, c0, n0, m0x, bias_row, gnorm_row, B, T, L):
    nt = T // L
    H, DK, DV = ML_HEADS, ML_DK, ML_DV
    BB = _seqs_per_step(B, ML_SEQS_PER_STEP)
    tok = lambda b, t: (b, t, 0)
    per_seq3 = lambda b, t: (b, 0, 0)
    per_seq4 = lambda b, t: (b, 0, 0, 0)
    const2 = lambda b, t: (0, 0)
    return pl.pallas_call(
        functools.partial(_mlstm_kernel, L=L, BB=BB),
        grid=(B // BB, nt),
        in_specs=[
            pl.BlockSpec((BB, L, ML_KW), tok),
            pl.BlockSpec((BB, L, ML_KW), tok),
            pl.BlockSpec((BB, L, ML_VW), tok),
            pl.BlockSpec((BB, L, ML_VW), tok),
            pl.BlockSpec((BB, L, LANES), tok),
            pl.BlockSpec((BB, H, DK, DV), per_seq4),
            pl.BlockSpec((BB, H, DK), per_seq3),
            pl.BlockSpec((BB, H, LANES), per_seq3),
            pl.BlockSpec((1, LANES), const2),
            pl.BlockSpec((1, ML_VW), const2),
        ],
        out_specs=[
            pl.BlockSpec((BB, L, ML_VW), tok),
            pl.BlockSpec((BB, H, DK, DV), per_seq4),
            pl.BlockSpec((BB, H, DK), per_seq3),
            pl.BlockSpec((BB, H, LANES), per_seq3),
        ],
        out_shape=[
            jax.ShapeDtypeStruct((B, T, ML_VW), BF16),
            jax.ShapeDtypeStruct((B, H, DK, DV), F32),
            jax.ShapeDtypeStruct((B, H, DK), F32),
            jax.ShapeDtypeStruct((B, H, LANES), F32),
        ],
        scratch_shapes=[
            pltpu.VMEM((BB, H, DK, DV), F32),
            pltpu.VMEM((BB, H, DK), F32),
            pltpu.VMEM((BB, H, LANES), F32),
        ],
        compiler_params=pltpu.CompilerParams(
            dimension_semantics=("arbitrary", "arbitrary"), vmem_limit_bytes=VMEM_LIMIT),
        name="mlstm_recurrence",
    )(q, k, v, o_pre, gates, c0, n0, m0x, bias_row, gnorm_row)


def _out_ffn_kernel(x_ref, o_ref, wo_ref, g_ref, wgu_ref, wd_ref, gfin_ref, y_ref, *, final):
    x1 = x_ref[...] + jnp.dot(o_ref[...], wo_ref[...], preferred_element_type=F32)
    hn = _rms(x1, g_ref[...]).astype(BF16)
    acc = x1
    for c in range(FFN_HIDDEN // FFN_CHUNK):
        lo = c * FFN_CHUNK
        gate = jnp.dot(hn, wgu_ref[:, lo:lo + FFN_CHUNK], preferred_element_type=F32)
        up = jnp.dot(hn, wgu_ref[:, FFN_HIDDEN + lo:FFN_HIDDEN + lo + FFN_CHUNK],
                     preferred_element_type=F32)
        act = (gate * _sigmoid(gate) * up).astype(BF16)
        acc = acc + jnp.dot(act, wd_ref[lo:lo + FFN_CHUNK, :], preferred_element_type=F32)
    if final:
        acc = _rms(acc, gfin_ref[...])
    y_ref[...] = acc


def _out_ffn(x, o, w_out, g, w_gu, w_down, g_final, final):
    n, d = x.shape
    tm = min(TOKEN_TILE, n)
    assert n % tm == 0 and FFN_HIDDEN % FFN_CHUNK == 0 and o.dtype == BF16
    tile = lambda i: (i, 0)
    const = lambda i: (0, 0)
    return pl.pallas_call(
        functools.partial(_out_ffn_kernel, final=final),
        grid=(n // tm,),
        in_specs=[
            pl.BlockSpec((tm, d), tile),
            pl.BlockSpec((tm, o.shape[1]), tile),
            pl.BlockSpec(w_out.shape, const),
            pl.BlockSpec((1, d), const),
            pl.BlockSpec(w_gu.shape, const),
            pl.BlockSpec(w_down.shape, const),
            pl.BlockSpec((1, d), const),
        ],
        out_specs=pl.BlockSpec((tm, d), tile),
        out_shape=jax.ShapeDtypeStruct((n, d), F32),
        compiler_params=pltpu.CompilerParams(
            dimension_semantics=("arbitrary",), vmem_limit_bytes=VMEM_LIMIT),
        name="out_ffn",
    )(x, o, w_out, g, w_gu, w_down, g_final)


def _pad_lanes(row, offset=0):
    out = jnp.zeros((1, LANES), F32)
    return lax.dynamic_update_slice(out, row.astype(F32)[None, :], (0, offset))


def _prep_weights(norm_mix, gdn_w_in, gdn_conv_w, gdn_a_log, gdn_dt_bias, gdn_norm, gdn_w_out,
                  ml_w_in, ml_b_i, ml_b_f, ml_norm, ml_w_out, norm_ffn, ffn_w_gu, ffn_w_down,
                  norm_final):
    H = GDN_HEADS
    w0 = gdn_w_in[0]
    gate_pad = jnp.zeros((D_MODEL, LANES - 2 * H), F32)
    w1 = ml_w_in[0]
    p = {
        "g_mix0": norm_mix[0][None, :], "g_mix1": norm_mix[1][None, :],
        "g_ffn0": norm_ffn[0][None, :], "g_ffn1": norm_ffn[1][None, :],
        "g_final": norm_final[None, :],
        "gdn_w_qkv": w0[:, :GDN_QKV].astype(BF16),
        "gdn_w_z": w0[:, GDN_QKV:GDN_QKV + GDN_VW].astype(BF16),
        "gdn_w_gates": jnp.concatenate([w0[:, GDN_QKV + GDN_VW:], gate_pad], axis=1).astype(BF16),
        "gdn_conv_w": gdn_conv_w[0],
        "gdn_alog": _pad_lanes(gdn_a_log[0]), "gdn_dtb": _pad_lanes(gdn_dt_bias[0]),
        "gdn_norm": gdn_norm[0][None, :],
        "gdn_w_out": gdn_w_out[0].astype(BF16),
        "ml_w_q": w1[:, :ML_KW].astype(BF16),
        "ml_w_k": w1[:, ML_KW:2 * ML_KW].astype(BF16),
        "ml_w_v": w1[:, 2 * ML_KW:2 * ML_KW + ML_VW].astype(BF16),
        "ml_w_o": w1[:, 2 * ML_KW + ML_VW:2 * ML_KW + 2 * ML_VW].astype(BF16),
        "ml_w_gates": jnp.concatenate([w1[:, 2 * ML_KW + 2 * ML_VW:], gate_pad], axis=1).astype(BF16),
        "ml_bias": _pad_lanes(jnp.concatenate([ml_b_i[0], ml_b_f[0]])),
        "ml_norm": ml_norm[0][None, :],
        "ml_w_out": ml_w_out[0].astype(BF16),
        "ffn_w_gu0": ffn_w_gu[0].astype(BF16), "ffn_w_gu1": ffn_w_gu[1].astype(BF16),
        "ffn_w_down0": ffn_w_down[0].astype(BF16), "ffn_w_down1": ffn_w_down[1].astype(BF16),
    }
    return p


def _trunk(x, conv0, s0, c0, n0, m0, L, p):
    B, T, D = x.shape
    N = B * T
    x2 = x.reshape(N, D)
    seq = lambda a: a.reshape(B, T, a.shape[-1])

    qkv, z, gates = _norm_proj(x2, p["g_mix0"], [p["gdn_w_qkv"], p["gdn_w_z"], p["gdn_w_gates"]])
    prev8 = jnp.concatenate(
        [jnp.zeros((B, SUBLANES - (CONV_W - 1), GDN_QKV), F32), conv0.astype(F32)], axis=1)
    o, s_fin = _gdn_recurrence(seq(qkv), seq(z), seq(gates), prev8, s0, p["gdn_conv_w"],
                               p["gdn_alog"], p["gdn_dtb"], p["gdn_norm"], B, T, L)
    conv_fin = seq(qkv)[:, T - (CONV_W - 1):, :]
    x2 = _out_ffn(x2, o.reshape(N, GDN_VW), p["gdn_w_out"], p["g_ffn0"], p["ffn_w_gu0"],
                  p["ffn_w_down0"], p["g_final"], final=False)

    q, k, v, o_pre, gates = _norm_proj(
        x2, p["g_mix1"], [p["ml_w_q"], p["ml_w_k"], p["ml_w_v"], p["ml_w_o"], p["ml_w_gates"]])
    m0x = jnp.broadcast_to(m0[:, :, None], (B, ML_HEADS, LANES))
    o, c_fin, n_fin, m_fin = _mlstm_recurrence(seq(q), seq(k), seq(v), seq(o_pre), seq(gates), c0, n0,
                                               m0x, p["ml_bias"], p["ml_norm"], B, T, L)
    y = _out_ffn(x2, o.reshape(N, ML_VW), p["ml_w_out"], p["g_ffn1"], p["ffn_w_gu1"],
                 p["ffn_w_down1"], p["g_final"], final=True)
    return (y.reshape(B, T, D), conv_fin[None], s_fin[None], c_fin[None], n_fin[None],
            m_fin[:, :, 0][None])


def kernel(x_prompt, x_sample, state_gdn_conv, state_gdn_S, state_mlstm_C, state_mlstm_n,
           state_mlstm_m, norm_mix, gdn_w_in, gdn_conv_w, gdn_a_log, gdn_dt_bias, gdn_norm,
           gdn_w_out, ml_w_in, ml_b_i, ml_b_f, ml_norm, ml_w_out, norm_ffn, ffn_w_gu,
           ffn_w_down, norm_final):
    p = _prep_weights(norm_mix, gdn_w_in, gdn_conv_w, gdn_a_log, gdn_dt_bias, gdn_norm, gdn_w_out,
                      ml_w_in, ml_b_i, ml_b_f, ml_norm, ml_w_out, norm_ffn, ffn_w_gu, ffn_w_down,
                      norm_final)
    Bp = x_prompt.shape[0]
    prompt = _trunk(
        x_prompt,
        jnp.zeros((Bp, CONV_W - 1, GDN_QKV), F32),
        jnp.zeros((Bp, GDN_HEADS, GDN_DK, GDN_DV), F32),
        jnp.zeros((Bp, ML_HEADS, ML_DK, ML_DV), F32),
        jnp.zeros((Bp, ML_HEADS, ML_DK), F32),
        jnp.zeros((Bp, ML_HEADS), F32),
        PROMPT_CHUNK, p)
    sample = _trunk(
        x_sample, state_gdn_conv[0], state_gdn_S[0], state_mlstm_C[0], state_mlstm_n[0],
        state_mlstm_m[0], x_sample.shape[1], p)
    return (prompt[0], sample[0]) + prompt[1:] + sample[1:]
```

```python
import functools

import jax
import jax.numpy as jnp
from jax import lax
from jax.experimental import pallas as pl
from jax.experimental.pallas import tpu as pltpu

F32 = jnp.float32
BF16 = jnp.bfloat16

D_MODEL = 1024
PROMPT_CHUNK = 64
GDN_HEADS = 8
GDN_DK = 128
GDN_DV = 128
GDN_KW = GDN_HEADS * GDN_DK
GDN_VW = GDN_HEADS * GDN_DV
GDN_QKV = 2 * GDN_KW + GDN_VW
CONV_W = 4
ML_HEADS = 8
ML_DK = 64
ML_DV = 128
ML_KW = ML_HEADS * ML_DK
ML_VW = ML_HEADS * ML_DV
FFN_HIDDEN = 2816
RMS_EPS = 1e-6
L2_EPS = 1e-6
NEG = -1e30

LANES = 128
SUBLANES = 8
TOKEN_TILE = 512
FFN_CHUNK = 256
GDN_SEQS_PER_STEP = 4
ML_SEQS_PER_STEP = 4
VMEM_LIMIT = 56 * 1024 * 1024

NN = ((1,), (0,))
NT = ((1,), (1,))
TN = ((0,), (0,))


def _dot(a, b, dims=NN):
    return lax.dot_general(a.astype(BF16), b.astype(BF16), (dims, ((), ())),
                           preferred_element_type=F32)


def _dot_sel(sel, x, dims=NN):
    hi = x.astype(BF16)
    r1 = x - hi.astype(F32)
    mid = r1.astype(BF16)
    lo = (r1 - mid.astype(F32)).astype(BF16)
    d = lambda p: lax.dot_general(sel, p, (dims, ((), ())), preferred_element_type=F32)
    return d(hi) + d(mid) + d(lo)


def _sigmoid(x):
    return 1.0 / (1.0 + jnp.exp(-x))


def _softplus(x):
    return jnp.maximum(x, 0.0) + jnp.log1p(jnp.exp(-jnp.abs(x)))


def _rms(x, g):
    return x * lax.rsqrt(jnp.mean(x * x, axis=-1, keepdims=True) + RMS_EPS) * g


def _masks(L):
    ri = lax.broadcasted_iota(jnp.int32, (L, L), 0)
    ci = lax.broadcasted_iota(jnp.int32, (L, L), 1)
    return ri >= ci, ri > ci


def _bcast_col(x, c, width):
    return jnp.broadcast_to(x[:, c:c + 1], (x.shape[0], width))


def _norm_proj_kernel(x_ref, g_ref, *refs, n_out):
    w_refs, o_refs = refs[:n_out], refs[n_out:]
    h = _rms(x_ref[...], g_ref[...]).astype(BF16)
    for w_ref, o_ref in zip(w_refs, o_refs):
        o_ref[...] = jnp.dot(h, w_ref[...], preferred_element_type=F32).astype(o_ref.dtype)


def _norm_proj(x, g, weights):
    n, d = x.shape
    tm = min(TOKEN_TILE, n)
    assert n % tm == 0
    in_specs = [pl.BlockSpec((tm, d), lambda i: (i, 0)),
                pl.BlockSpec((1, d), lambda i: (0, 0))]
    in_specs += [pl.BlockSpec(w.shape, lambda i: (0, 0)) for w in weights]
    out_specs = [pl.BlockSpec((tm, w.shape[1]), lambda i: (i, 0)) for w in weights]
    out_shape = [jax.ShapeDtypeStruct((n, w.shape[1]), F32) for w in weights]
    return pl.pallas_call(
        functools.partial(_norm_proj_kernel, n_out=len(weights)),
        grid=(n // tm,),
        in_specs=in_specs, out_specs=out_specs, out_shape=out_shape,
        compiler_params=pltpu.CompilerParams(
            dimension_semantics=("arbitrary",), vmem_limit_bytes=VMEM_LIMIT),
        name="norm_proj",
    )(x, g, *weights)


def _unit_lower_inverses(a_list, L):
    ri = lax.broadcasted_iota(jnp.int32, (L, L), 0)
    ci = lax.broadcasted_iota(jnp.int32, (L, L), 1)
    eye = jnp.where(ri == ci, 1.0, 0.0)
    ms = [-a for a in a_list]
    ps = [eye + m for m in ms]
    mb = [m.astype(BF16) for m in ms]
    ms = [_dot(m, m) for m in mb]
    span = 2
    while span < L:
        mb = [m.astype(BF16) for m in ms]
        pb = [p.astype(BF16) for p in ps]
        if 2 * span >= L:
            ps = [p + _dot(b, m) for p, b, m in zip(ps, pb, mb)]
        else:
            rs = [_dot(jnp.concatenate([b, m], axis=0), m) for b, m in zip(pb, mb)]
            ps = [p + r[:L] for p, r in zip(ps, rs)]
            ms = [r[L:] for r in rs]
        span *= 2
    return ps


def _gdn_kernel(qkv_ref, z_ref, gates_ref, prev_ref, s0_ref, convw_ref, alog_ref, dtb_ref, gn_ref,
                o_ref, sout_ref, tail_scr, s_scr, *, L, BB):
    H, DK, DV, KW = GDN_HEADS, GDN_DK, GDN_DV, GDN_KW
    t = pl.program_id(1)

    @pl.when(t == 0)
    def _():
        tail_scr[...] = prev_ref[...]
        s_scr[...] = s0_ref[...]

    causal, strict = _masks(L)
    tril = jnp.where(causal, 1.0, 0.0).astype(BF16)
    cw = convw_ref[...]
    gn = gn_ref[...]
    tail_row = lax.broadcasted_iota(jnp.int32, (SUBLANES, GDN_QKV), 0)
    sum_mat = jnp.ones((DK, DK), BF16)
    mean_mat = jnp.full((DV, DV), 1.0 / DV, BF16)

    y, dec, dec_t, beta_all = [], [], [], []
    for b in range(BB):
        x = qkv_ref[b]
        tail = tail_scr[b]
        yb = x * cw[CONV_W - 1:CONV_W]
        for s in range(1, CONV_W):
            xs = pltpu.roll(x, s, 0)
            head = jnp.where(tail_row < s, pltpu.roll(tail, s, 0), xs[:SUBLANES])
            xs = jnp.concatenate([head, xs[SUBLANES:]], axis=0)
            yb = yb + xs * cw[CONV_W - 1 - s:CONV_W - s]
        tail_scr[b] = x[L - SUBLANES:L]
        y.append(yb * _sigmoid(yb))
        gt = gates_ref[b]
        g_all = -jnp.exp(alog_ref[...]) * _softplus(gt + dtb_ref[...])
        beta_all.append(_sigmoid(gt))
        dec.append(_dot_sel(tril, g_all))
        dec_t.append(dec[b].T)

    items = [(b, h) for b in range(BB) for h in range(H)]
    idx = range(len(items))
    q, k, v = [], [], []
    for b, h in items:
        qh = y[b][:, h * DK:(h + 1) * DK]
        kh = y[b][:, KW + h * DK:KW + (h + 1) * DK]
        ss = _dot(jnp.concatenate([qh * qh, kh * kh], axis=0), sum_mat)
        q.append(qh * lax.rsqrt(ss[:L] + L2_EPS) * (DK ** -0.5))
        k.append(kh * lax.rsqrt(ss[L:] + L2_EPS))
        v.append(y[b][:, 2 * KW + h * DV:2 * KW + (h + 1) * DV])

    dcol = [_bcast_col(dec[b], h, DK) for b, h in items]
    bcol = [_bcast_col(beta_all[b], H + h, DK) for b, h in items]
    gamma = []
    for i, (b, h) in enumerate(items):
        diff = dcol[i][:, :L] - dec_t[b][h:h + 1, :]
        gamma.append(jnp.where(causal, jnp.exp(jnp.where(causal, diff, 0.0)), 0.0))

    qk = [_dot(jnp.concatenate([q[i], k[i]], axis=0), k[i], NT) for i in idx]
    attn = [qk[i][:L] * gamma[i] for i in idx]
    a = [jnp.where(strict, bcol[i][:, :L] * qk[i][L:] * gamma[i], 0.0) for i in idx]
    tm = _unit_lower_inverses(a, L)

    edec = [jnp.exp(dcol[i]) for i in idx]
    uw = [_dot(tm[i], jnp.concatenate([v[i] * bcol[i], k[i] * (bcol[i] * edec[i])], axis=1))
          for i in idx]
    dlast = [dcol[i][L - 1:L, :] for i in idx]
    k_tail = [k[i] * jnp.exp(dlast[i] - dcol[i]) for i in idx]

    s = [s_scr[b, h] for b, h in items]
    wq = [_dot(jnp.concatenate([uw[i][:, DV:], q[i] * edec[i]], axis=0), s[i]) for i in idx]
    v_new = [uw[i][:, :DV] - wq[i][:L] for i in idx]
    o_intra = [_dot(attn[i], v_new[i]) for i in idx]
    ds = [_dot(k_tail[i], v_new[i], TN) for i in idx]
    for i, (b, h) in enumerate(items):
        s_scr[b, h] = s[i] * jnp.exp(dlast[i]) + ds[i]
    o = [wq[i][L:] + o_intra[i] for i in idx]
    ms = [_dot(o[i] * o[i], mean_mat) for i in idx]
    for i, (b, h) in enumerate(items):
        zh = z_ref[b, :, h * DV:(h + 1) * DV]
        o_ref[b, :, h * DV:(h + 1) * DV] = (
            o[i] * lax.rsqrt(ms[i] + RMS_EPS) * gn * (zh * _sigmoid(zh))).astype(o_ref.dtype)

    @pl.when(t == pl.num_programs(1) - 1)
    def _():
        sout_ref[...] = s_scr[...]


def _seqs_per_step(B, want):
    bb = min(B, want)
    assert B % bb == 0
    return bb


def _gdn_recurrence(qkv, z, gates, prev8, s0, conv_w, alog_row, dtb_row, gnorm_row, B, T, L):
    nt = T // L
    H, DK, DV = GDN_HEADS, GDN_DK, GDN_DV
    BB = _seqs_per_step(B, GDN_SEQS_PER_STEP)
    tok = lambda b, t: (b, t, 0)
    per_seq3 = lambda b, t: (b, 0, 0)
    per_seq4 = lambda b, t: (b, 0, 0, 0)
    const2 = lambda b, t: (0, 0)
    return pl.pallas_call(
        functools.partial(_gdn_kernel, L=L, BB=BB),
        grid=(B // BB, nt),
        in_specs=[
            pl.BlockSpec((BB, L, GDN_QKV), tok),
            pl.BlockSpec((BB, L, GDN_VW), tok),
            pl.BlockSpec((BB, L, LANES), tok),
            pl.BlockSpec((BB, SUBLANES, GDN_QKV), per_seq3),
            pl.BlockSpec((BB, H, DK, DV), per_seq4),
            pl.BlockSpec((CONV_W, GDN_QKV), const2),
            pl.BlockSpec((1, LANES), const2),
            pl.BlockSpec((1, LANES), const2),
            pl.BlockSpec((1, DV), const2),
        ],
        out_specs=[
            pl.BlockSpec((BB, L, GDN_VW), tok),
            pl.BlockSpec((BB, H, DK, DV), per_seq4),
        ],
        out_shape=[
            jax.ShapeDtypeStruct((B, T, GDN_VW), BF16),
            jax.ShapeDtypeStruct((B, H, DK, DV), F32),
        ],
        scratch_shapes=[
            pltpu.VMEM((BB, SUBLANES, GDN_QKV), F32),
            pltpu.VMEM((BB, H, DK, DV), F32),
        ],
        compiler_params=pltpu.CompilerParams(
            dimension_semantics=("arbitrary", "arbitrary"), vmem_limit_bytes=VMEM_LIMIT),
        name="gdn_recurrence",
    )(qkv, z, gates, prev8, s0, conv_w, alog_row, dtb_row, gnorm_row)


def _mlstm_kernel(q_ref, k_ref, v_ref, op_ref, gates_ref, cx0_ref, m0_ref, bias_ref, gn_ref,
                  o_ref, cxout_ref, mout_ref, cx_scr, m_scr, *, L, BB):
    H, DK, DV = ML_HEADS, ML_DK, ML_DV
    t = pl.program_id(1)

    @pl.when(t == 0)
    def _():
        cx_scr[...] = cx0_ref[...]
        m_scr[...] = m0_ref[...]

    causal, _ = _masks(L)
    tril = jnp.where(causal, 1.0, 0.0).astype(BF16)
    ones_v = jnp.ones((L, DV), BF16)
    mean_mat = jnp.full((DV, DV), 1.0 / DV, BF16)
    bcum, d_all, d_t = [], [], []
    for b in range(BB):
        pre = gates_ref[b] + bias_ref[...]
        bcum.append(_dot_sel(tril, -_softplus(-pre)))
        d_all.append(pre - pltpu.roll(bcum[b], LANES - H, 1))
        d_t.append(d_all[b].T)

    items = [(b, h) for b in range(BB) for h in range(H)]
    idx = range(len(items))
    q = [q_ref[b, :, h * DK:(h + 1) * DK].astype(BF16) for b, h in items]
    k = [k_ref[b, :, h * DK:(h + 1) * DK] * (DK ** -0.5) for b, h in items]
    vx = [jnp.concatenate([v_ref[b, :, h * DV:(h + 1) * DV].astype(BF16), ones_v], axis=1)
          for b, h in items]
    cx = [cx_scr[b, h] for b, h in items]
    qk = [_dot(q[i], k[i], NT) for i in idx]
    qcx = [_dot(q[i], cx[i]) for i in idx]

    bcol = [_bcast_col(bcum[b], H + h, DV) for b, h in items]
    dcol = [_bcast_col(d_all[b], h, DV) for b, h in items]
    m_intra, s_hat = [], []
    for i, (b, h) in enumerate(items):
        log_d = jnp.where(causal, bcol[i][:, :L] + d_t[b][h:h + 1, :], NEG)
        mi = jnp.max(log_d, axis=-1, keepdims=True)
        m_intra.append(mi)
        s_hat.append(qk[i] * jnp.exp(log_d - mi))
    nhx = [_dot(s_hat[i], vx[i]) for i in idx]

    blast = [bcol[i][L - 1:L, :] for i in idx]
    m_w, kw = [], []
    for i in idx:
        log_w = blast[i] + dcol[i]
        mw = jnp.max(log_w, axis=0, keepdims=True)
        m_w.append(mw)
        kw.append(k[i] * jnp.exp(log_w - mw)[:, :DK])
    ckvx = [_dot(kw[i], vx[i], TN) for i in idx]

    m = [m_scr[b, h:h + 1, :] for b, h in items]
    hcell = []
    for i in idx:
        log_inter = bcol[i] + m[i]
        m_t = jnp.maximum(log_inter, m_intra[i])
        s_inter = jnp.exp(log_inter - m_t)
        s_intra = jnp.exp(m_intra[i] - m_t)
        num = s_inter * qcx[i][:, :DV] + s_intra * nhx[i][:, :DV]
        qn = s_inter * qcx[i][:, DV:] + s_intra * nhx[i][:, DV:]
        hcell.append(num / jnp.maximum(jnp.abs(qn), jnp.exp(-m_t)))
    ms = [_dot(hcell[i] * hcell[i], mean_mat) for i in idx]
    for i, (b, h) in enumerate(items):
        oh = op_ref[b, :, h * DV:(h + 1) * DV]
        o_ref[b, :, h * DV:(h + 1) * DV] = (
            hcell[i] * lax.rsqrt(ms[i] + RMS_EPS) * gn_ref[:, h * DV:(h + 1) * DV]
            * _sigmoid(oh)).astype(o_ref.dtype)

    for i, (b, h) in enumerate(items):
        m_new = jnp.maximum(blast[i] + m[i], m_w[i])
        a_old = jnp.exp(blast[i] + m[i] - m_new)
        a_new = jnp.exp(m_w[i] - m_new)
        cx_scr[b, h] = (jnp.concatenate([a_old, a_old], axis=1) * cx[i]
                        + jnp.concatenate([a_new, a_new], axis=1) * ckvx[i])
        m_scr[b, h:h + 1, :] = m_new

    @pl.when(t == pl.num_programs(1) - 1)
    def _():
        cxout_ref[...] = cx_scr[...]
        mout_ref[...] = m_scr[...]


def _mlstm_recurrence(q, k, v, o_pre, gates, cx0, m0x, bias_row, gnorm_row, B, T, L):
    nt = T // L
    H, DK, DV = ML_HEADS, ML_DK, ML_DV
    BB = _seqs_per_step(B, ML_SEQS_PER_STEP)
    tok = lambda b, t: (b, t, 0)
    per_seq3 = lambda b, t: (b, 0, 0)
    per_seq4 = lambda b, t: (b, 0, 0, 0)
    const2 = lambda b, t: (0, 0)
    return pl.pallas_call(
        functools.partial(_mlstm_kernel, L=L, BB=BB),
        grid=(B // BB, nt),
        in_specs=[
            pl.BlockSpec((BB, L, ML_KW), tok),
            pl.BlockSpec((BB, L, ML_KW), tok),
            pl.BlockSpec((BB, L, ML_VW), tok),
            pl.BlockSpec((BB, L, ML_VW), tok),
            pl.BlockSpec((BB, L, LANES), tok),
            pl.BlockSpec((BB, H, DK, 2 * DV), per_seq4),
            pl.BlockSpec((BB, H, LANES), per_seq3),
            pl.BlockSpec((1, LANES), const2),
            pl.BlockSpec((1, ML_VW), const2),
        ],
        out_specs=[
            pl.BlockSpec((BB, L, ML_VW), tok),
            pl.BlockSpec((BB, H, DK, 2 * DV), per_seq4),
            pl.BlockSpec((BB, H, LANES), per_seq3),
        ],
        out_shape=[
            jax.ShapeDtypeStruct((B, T, ML_VW), BF16),
            jax.ShapeDtypeStruct((B, H, DK, 2 * DV), F32),
            jax.ShapeDtypeStruct((B, H, LANES), F32),
        ],
        scratch_shapes=[
            pltpu.VMEM((BB, H, DK, 2 * DV), F32),
            pltpu.VMEM((BB, H, LANES), F32),
        ],
        compiler_params=pltpu.CompilerParams(
            dimension_semantics=("arbitrary", "arbitrary"), vmem_limit_bytes=VMEM_LIMIT),
        name="mlstm_recurrence",
    )(q, k, v, o_pre, gates, cx0, m0x, bias_row, gnorm_row)


def _out_ffn_kernel(x_ref, o_ref, wo_ref, g_ref, wgu_ref, wd_ref, gfin_ref, y_ref, *, final):
    x1 = x_ref[...] + jnp.dot(o_ref[...], wo_ref[...], preferred_element_type=F32)
    hn = _rms(x1, g_ref[...]).astype(BF16)
    acc = x1
    for c in range(FFN_HIDDEN // FFN_CHUNK):
        lo = c * FFN_CHUNK
        gate = jnp.dot(hn, wgu_ref[:, lo:lo + FFN_CHUNK], preferred_element_type=F32)
        up = jnp.dot(hn, wgu_ref[:, FFN_HIDDEN + lo:FFN_HIDDEN + lo + FFN_CHUNK],
                     preferred_element_type=F32)
        act = (gate * _sigmoid(gate) * up).astype(BF16)
        acc = acc + jnp.dot(act, wd_ref[lo:lo + FFN_CHUNK, :], preferred_element_type=F32)
    if final:
        acc = _rms(acc, gfin_ref[...])
    y_ref[...] = acc


def _out_ffn(x, o, w_out, g, w_gu, w_down, g_final, final):
    n, d = x.shape
    tm = min(TOKEN_TILE, n)
    assert n % tm == 0 and FFN_HIDDEN % FFN_CHUNK == 0 and o.dtype == BF16
    tile = lambda i: (i, 0)
    const = lambda i: (0, 0)
    return pl.pallas_call(
        functools.partial(_out_ffn_kernel, final=final),
        grid=(n // tm,),
        in_specs=[
            pl.BlockSpec((tm, d), tile),
            pl.BlockSpec((tm, o.shape[1]), tile),
            pl.BlockSpec(w_out.shape, const),
            pl.BlockSpec((1, d), const),
            pl.BlockSpec(w_gu.shape, const),
            pl.BlockSpec(w_down.shape, const),
            pl.BlockSpec((1, d), const),
        ],
        out_specs=pl.BlockSpec((tm, d), tile),
        out_shape=jax.ShapeDtypeStruct((n, d), F32),
        compiler_params=pltpu.CompilerParams(
            dimension_semantics=("arbitrary",), vmem_limit_bytes=VMEM_LIMIT),
        name="out_ffn",
    )(x, o, w_out, g, w_gu, w_down, g_final)


def _pad_lanes(row, offset=0):
    out = jnp.zeros((1, LANES), F32)
    return lax.dynamic_update_slice(out, row.astype(F32)[None, :], (0, offset))


def _prep_weights(norm_mix, gdn_w_in, gdn_conv_w, gdn_a_log, gdn_dt_bias, gdn_norm, gdn_w_out,
                  ml_w_in, ml_b_i, ml_b_f, ml_norm, ml_w_out, norm_ffn, ffn_w_gu, ffn_w_down,
                  norm_final):
    H = GDN_HEADS
    w0 = gdn_w_in[0]
    gate_pad = jnp.zeros((D_MODEL, LANES - 2 * H), F32)
    w1 = ml_w_in[0]
    p = {
        "g_mix0": norm_mix[0][None, :], "g_mix1": norm_mix[1][None, :],
        "g_ffn0": norm_ffn[0][None, :], "g_ffn1": norm_ffn[1][None, :],
        "g_final": norm_final[None, :],
        "gdn_w_qkv": w0[:, :GDN_QKV].astype(BF16),
        "gdn_w_z": w0[:, GDN_QKV:GDN_QKV + GDN_VW].astype(BF16),
        "gdn_w_gates": jnp.concatenate([w0[:, GDN_QKV + GDN_VW:], gate_pad], axis=1).astype(BF16),
        "gdn_conv_w": gdn_conv_w[0],
        "gdn_alog": _pad_lanes(gdn_a_log[0]), "gdn_dtb": _pad_lanes(gdn_dt_bias[0]),
        "gdn_norm": gdn_norm[0][None, :],
        "gdn_w_out": gdn_w_out[0].astype(BF16),
        "ml_w_q": w1[:, :ML_KW].astype(BF16),
        "ml_w_k": w1[:, ML_KW:2 * ML_KW].astype(BF16),
        "ml_w_v": w1[:, 2 * ML_KW:2 * ML_KW + ML_VW].astype(BF16),
        "ml_w_o": w1[:, 2 * ML_KW + ML_VW:2 * ML_KW + 2 * ML_VW].astype(BF16),
        "ml_w_gates": jnp.concatenate([w1[:, 2 * ML_KW + 2 * ML_VW:], gate_pad], axis=1).astype(BF16),
        "ml_bias": _pad_lanes(jnp.concatenate([ml_b_i[0], ml_b_f[0]])),
        "ml_norm": ml_norm[0][None, :],
        "ml_w_out": ml_w_out[0].astype(BF16),
        "ffn_w_gu0": ffn_w_gu[0].astype(BF16), "ffn_w_gu1": ffn_w_gu[1].astype(BF16),
        "ffn_w_down0": ffn_w_down[0].astype(BF16), "ffn_w_down1": ffn_w_down[1].astype(BF16),
    }
    return p


def _trunk(x, conv0, s0, c0, n0, m0, L, p):
    B, T, D = x.shape
    N = B * T
    x2 = x.reshape(N, D)
    seq = lambda a: a.reshape(B, T, a.shape[-1])

    qkv, z, gates = _norm_proj(x2, p["g_mix0"], [p["gdn_w_qkv"], p["gdn_w_z"], p["gdn_w_gates"]])
    prev8 = jnp.concatenate(
        [jnp.zeros((B, SUBLANES - (CONV_W - 1), GDN_QKV), F32), conv0.astype(F32)], axis=1)
    o, s_fin = _gdn_recurrence(seq(qkv), seq(z), seq(gates), prev8, s0, p["gdn_conv_w"],
                               p["gdn_alog"], p["gdn_dtb"], p["gdn_norm"], B, T, L)
    conv_fin = seq(qkv)[:, T - (CONV_W - 1):, :]
    x2 = _out_ffn(x2, o.reshape(N, GDN_VW), p["gdn_w_out"], p["g_ffn0"], p["ffn_w_gu0"],
                  p["ffn_w_down0"], p["g_final"], final=False)

    q, k, v, o_pre, gates = _norm_proj(
        x2, p["g_mix1"], [p["ml_w_q"], p["ml_w_k"], p["ml_w_v"], p["ml_w_o"], p["ml_w_gates"]])
    m0x = jnp.broadcast_to(m0[:, :, None], (B, ML_HEADS, LANES))
    cx0 = jnp.concatenate(
        [c0, jnp.broadcast_to(n0[:, :, :, None], (B, ML_HEADS, ML_DK, ML_DV))], axis=-1)
    o, cx_fin, m_fin = _mlstm_recurrence(seq(q), seq(k), seq(v), seq(o_pre), seq(gates), cx0, m0x,
                                         p["ml_bias"], p["ml_norm"], B, T, L)
    c_fin, n_fin = cx_fin[..., :ML_DV], cx_fin[..., ML_DV]
    y = _out_ffn(x2, o.reshape(N, ML_VW), p["ml_w_out"], p["g_ffn1"], p["ffn_w_gu1"],
                 p["ffn_w_down1"], p["g_final"], final=True)
    return (y.reshape(B, T, D), conv_fin[None], s_fin[None], c_fin[None], n_fin[None],
            m_fin[:, :, 0][None])


def kernel(x_prompt, x_sample, state_gdn_conv, state_gdn_S, state_mlstm_C, state_mlstm_n,
           state_mlstm_m, norm_mix, gdn_w_in, gdn_conv_w, gdn_a_log, gdn_dt_bias, gdn_norm,
           gdn_w_out, ml_w_in, ml_b_i, ml_b_f, ml_norm, ml_w_out, norm_ffn, ffn_w_gu,
           ffn_w_down, norm_final):
    p = _prep_weights(norm_mix, gdn_w_in, gdn_conv_w, gdn_a_log, gdn_dt_bias, gdn_norm, gdn_w_out,
                      ml_w_in, ml_b_i, ml_b_f, ml_norm, ml_w_out, norm_ffn, ffn_w_gu, ffn_w_down,
                      norm_final)
    Bp = x_prompt.shape[0]
    prompt = _trunk(
        x_prompt,
        jnp.zeros((Bp, CONV_W - 1, GDN_QKV), F32),
        jnp.zeros((Bp, GDN_HEADS, GDN_DK, GDN_DV), F32),
        jnp.zeros((Bp, ML_HEADS, ML_DK, ML_DV), F32),
        jnp.zeros((Bp, ML_HEADS, ML_DK), F32),
        jnp.zeros((Bp, ML_HEADS), F32),
        PROMPT_CHUNK, p)
    sample = _trunk(
        x_sample, state_gdn_conv[0], state_gdn_S[0], state_mlstm_C[0], state_mlstm_n[0],
        state_mlstm_m[0], x_sample.shape[1], p)
    return (prompt[0], sample[0]) + prompt[1:] + sample[1:]
```

```python
import functools

import jax
import jax.numpy as jnp
from jax import lax
from jax.experimental import pallas as pl
from jax.experimental.pallas import tpu as pltpu

F32 = jnp.float32
BF16 = jnp.bfloat16

D_MODEL = 1024
PROMPT_CHUNK = 64
GDN_HEADS = 8
GDN_DK = 128
GDN_DV = 128
GDN_KW = GDN_HEADS * GDN_DK
GDN_VW = GDN_HEADS * GDN_DV
GDN_QKV = 2 * GDN_KW + GDN_VW
CONV_W = 4
ML_HEADS = 8
ML_DK = 64
ML_DV = 128
ML_KW = ML_HEADS * ML_DK
ML_VW = ML_HEADS * ML_DV
FFN_HIDDEN = 2816
RMS_EPS = 1e-6
L2_EPS = 1e-6
NEG = -1e30

LANES = 128
SUBLANES = 8
TOKEN_TILE = 1024
FFN_TOKEN_TILE = 1024
FFN_CHUNK = 256
GDN_SEQS_PER_STEP = 4
ML_SEQS_PER_STEP = 4
ML_STAGES = 6
VMEM_LIMIT = 56 * 1024 * 1024

NN = ((1,), (0,))
NT = ((1,), (1,))
TN = ((0,), (0,))


def _dot(a, b, dims=NN):
    return lax.dot_general(a.astype(BF16), b.astype(BF16), (dims, ((), ())),
                           preferred_element_type=F32)


def _dot_sel(sel, x, dims=NN):
    hi = x.astype(BF16)
    r1 = x - hi.astype(F32)
    mid = r1.astype(BF16)
    lo = (r1 - mid.astype(F32)).astype(BF16)
    d = lambda p: lax.dot_general(sel, p, (dims, ((), ())), preferred_element_type=F32)
    return d(hi) + d(mid) + d(lo)


def _sigmoid(x):
    return 1.0 / (1.0 + jnp.exp(-x))


def _softplus(x):
    return jnp.maximum(x, 0.0) + jnp.log1p(jnp.exp(-jnp.abs(x)))


def _rms(x, g):
    return x * lax.rsqrt(jnp.mean(x * x, axis=-1, keepdims=True) + RMS_EPS) * g


def _masks(L):
    ri = lax.broadcasted_iota(jnp.int32, (L, L), 0)
    ci = lax.broadcasted_iota(jnp.int32, (L, L), 1)
    return ri >= ci, ri > ci


def _bcast_col(x, c, width):
    return jnp.broadcast_to(x[:, c:c + 1], (x.shape[0], width))


def _norm_proj_kernel(x_ref, g_ref, *refs, n_out):
    w_refs, o_refs = refs[:n_out], refs[n_out:]
    h = _rms(x_ref[...], g_ref[...]).astype(BF16)
    for w_ref, o_ref in zip(w_refs, o_refs):
        o_ref[...] = jnp.dot(h, w_ref[...], preferred_element_type=F32).astype(o_ref.dtype)


def _norm_proj(x, g, weights):
    n, d = x.shape
    tm = min(TOKEN_TILE, n)
    assert n % tm == 0
    in_specs = [pl.BlockSpec((tm, d), lambda i: (i, 0)),
                pl.BlockSpec((1, d), lambda i: (0, 0))]
    in_specs += [pl.BlockSpec(w.shape, lambda i: (0, 0), pipeline_mode=pl.Buffered(1))
                 for w in weights]
    out_specs = [pl.BlockSpec((tm, w.shape[1]), lambda i: (i, 0)) for w in weights]
    out_shape = [jax.ShapeDtypeStruct((n, w.shape[1]), F32) for w in weights]
    return pl.pallas_call(
        functools.partial(_norm_proj_kernel, n_out=len(weights)),
        grid=(n // tm,),
        in_specs=in_specs, out_specs=out_specs, out_shape=out_shape,
        compiler_params=pltpu.CompilerParams(
            dimension_semantics=("arbitrary",), vmem_limit_bytes=VMEM_LIMIT),
        name="norm_proj",
    )(x, g, *weights)


def _run_lockstep(stage_generators):
    active = list(stage_generators)
    while active:
        for g in list(active):
            try:
                next(g)
            except StopIteration:
                active.remove(g)


def _unit_lower_inverses(a_list, L):
    ri = lax.broadcasted_iota(jnp.int32, (L, L), 0)
    ci = lax.broadcasted_iota(jnp.int32, (L, L), 1)
    eye = jnp.where(ri == ci, 1.0, 0.0)
    ms = [-a for a in a_list]
    ps = [eye + m for m in ms]
    mb = [m.astype(BF16) for m in ms]
    ms = [_dot(m, m) for m in mb]
    span = 2
    while span < L:
        yield
        mb = [m.astype(BF16) for m in ms]
        pb = [p.astype(BF16) for p in ps]
        if 2 * span >= L:
            ps = [p + _dot(b, m) for p, b, m in zip(ps, pb, mb)]
        else:
            rs = [_dot(jnp.concatenate([b, m], axis=0), m) for b, m in zip(pb, mb)]
            ps = [p + r[:L] for p, r in zip(ps, rs)]
            ms = [r[L:] for r in rs]
        span *= 2
    return ps


def _gdn_kernel(qkv_ref, z_ref, gates_ref, prev_ref, s0_ref, convw_ref, alog_ref, dtb_ref, gn_ref,
                o_ref, sout_ref, tail_scr, s_scr, *, L, BB):
    H, DK, DV, KW = GDN_HEADS, GDN_DK, GDN_DV, GDN_KW
    t = pl.program_id(1)

    @pl.when(t == 0)
    def _():
        tail_scr[...] = prev_ref[...]
        s_scr[...] = s0_ref[...]

    causal, strict = _masks(L)
    tril = jnp.where(causal, 1.0, 0.0).astype(BF16)
    cw = convw_ref[...]
    gn = gn_ref[...]
    tail_row = lax.broadcasted_iota(jnp.int32, (SUBLANES, GDN_QKV), 0)
    sum_mat = jnp.ones((DK, DK), BF16)
    mean_mat = jnp.full((DV, DV), 1.0 / DV, BF16)
    heads = range(H)

    def sequence_stages(b):
        x = qkv_ref[b]
        tail = tail_scr[b]
        y = x * cw[CONV_W - 1:CONV_W]
        for s in range(1, CONV_W):
            xs = pltpu.roll(x, s, 0)
            head = jnp.where(tail_row < s, pltpu.roll(tail, s, 0), xs[:SUBLANES])
            xs = jnp.concatenate([head, xs[SUBLANES:]], axis=0)
            y = y + xs * cw[CONV_W - 1 - s:CONV_W - s]
        tail_scr[b] = x[L - SUBLANES:L]
        y = y * _sigmoid(y)
        gt = gates_ref[b]
        g_all = -jnp.exp(alog_ref[...]) * _softplus(gt + dtb_ref[...])
        beta_all = _sigmoid(gt)
        dec = _dot_sel(tril, g_all)
        dec_t = dec.T
        qr = [y[:, h * DK:(h + 1) * DK] for h in heads]
        kr = [y[:, KW + h * DK:KW + (h + 1) * DK] for h in heads]
        v = [y[:, 2 * KW + h * DV:2 * KW + (h + 1) * DV] for h in heads]
        ss = [_dot(jnp.concatenate([qr[h] * qr[h], kr[h] * kr[h]], axis=0), sum_mat) for h in heads]
        yield

        q = [qr[h] * lax.rsqrt(ss[h][:L] + L2_EPS) * (DK ** -0.5) for h in heads]
        k = [kr[h] * lax.rsqrt(ss[h][L:] + L2_EPS) for h in heads]
        qk = [_dot(jnp.concatenate([q[h], k[h]], axis=0), k[h], NT) for h in heads]
        dcol = [_bcast_col(dec, h, DK) for h in heads]
        bcol = [_bcast_col(beta_all, H + h, DK) for h in heads]
        gamma = []
        for h in heads:
            diff = dcol[h][:, :L] - dec_t[h:h + 1, :]
            gamma.append(jnp.where(causal, jnp.exp(jnp.where(causal, diff, 0.0)), 0.0))
        yield

        attn = [qk[h][:L] * gamma[h] for h in heads]
        a = [jnp.where(strict, bcol[h][:, :L] * qk[h][L:] * gamma[h], 0.0) for h in heads]
        tm = yield from _unit_lower_inverses(a, L)
        edec = [jnp.exp(dcol[h]) for h in heads]
        uw = [_dot(tm[h], jnp.concatenate([v[h] * bcol[h], k[h] * (bcol[h] * edec[h])], axis=1))
              for h in heads]
        dlast = [dcol[h][L - 1:L, :] for h in heads]
        k_tail = [k[h] * jnp.exp(dlast[h] - dcol[h]) for h in heads]
        q_dec = [q[h] * edec[h] for h in heads]
        yield

        s = [s_scr[b, h] for h in heads]
        wq = [_dot(jnp.concatenate([uw[h][:, DV:], q_dec[h]], axis=0), s[h]) for h in heads]
        yield

        v_new = [uw[h][:, :DV] - wq[h][:L] for h in heads]
        o_intra = [_dot(attn[h], v_new[h]) for h in heads]
        ds = [_dot(k_tail[h], v_new[h], TN) for h in heads]
        yield

        for h in heads:
            s_scr[b, h] = s[h] * jnp.exp(dlast[h]) + ds[h]
        o = [wq[h][L:] + o_intra[h] for h in heads]
        ms = [_dot(o[h] * o[h], mean_mat) for h in heads]
        yield

        for h in heads:
            zh = z_ref[b, :, h * DV:(h + 1) * DV]
            o_ref[b, :, h * DV:(h + 1) * DV] = (
                o[h] * lax.rsqrt(ms[h] + RMS_EPS) * gn * (zh * _sigmoid(zh))).astype(o_ref.dtype)

    _run_lockstep(sequence_stages(b) for b in range(BB))

    @pl.when(t == pl.num_programs(1) - 1)
    def _():
        sout_ref[...] = s_scr[...]


def _seqs_per_step(B, want):
    bb = min(B, want)
    assert B % bb == 0
    return bb


def _gdn_recurrence(qkv, z, gates, prev8, s0, conv_w, alog_row, dtb_row, gnorm_row, B, T, L):
    nt = T // L
    H, DK, DV = GDN_HEADS, GDN_DK, GDN_DV
    BB = _seqs_per_step(B, GDN_SEQS_PER_STEP)
    tok = lambda b, t: (b, t, 0)
    per_seq3 = lambda b, t: (b, 0, 0)
    per_seq4 = lambda b, t: (b, 0, 0, 0)
    const2 = lambda b, t: (0, 0)
    return pl.pallas_call(
        functools.partial(_gdn_kernel, L=L, BB=BB),
        grid=(B // BB, nt),
        in_specs=[
            pl.BlockSpec((BB, L, GDN_QKV), tok),
            pl.BlockSpec((BB, L, GDN_VW), tok),
            pl.BlockSpec((BB, L, LANES), tok),
            pl.BlockSpec((BB, SUBLANES, GDN_QKV), per_seq3),
            pl.BlockSpec((BB, H, DK, DV), per_seq4),
            pl.BlockSpec((CONV_W, GDN_QKV), const2),
            pl.BlockSpec((1, LANES), const2),
            pl.BlockSpec((1, LANES), const2),
            pl.BlockSpec((1, DV), const2),
        ],
        out_specs=[
            pl.BlockSpec((BB, L, GDN_VW), tok),
            pl.BlockSpec((BB, H, DK, DV), per_seq4),
        ],
        out_shape=[
            jax.ShapeDtypeStruct((B, T, GDN_VW), BF16),
            jax.ShapeDtypeStruct((B, H, DK, DV), F32),
        ],
        scratch_shapes=[
            pltpu.VMEM((BB, SUBLANES, GDN_QKV), F32),
            pltpu.VMEM((BB, H, DK, DV), F32),
        ],
        compiler_params=pltpu.CompilerParams(
            dimension_semantics=("arbitrary", "arbitrary"), vmem_limit_bytes=VMEM_LIMIT),
        name="gdn_recurrence",
    )(qkv, z, gates, prev8, s0, conv_w, alog_row, dtb_row, gnorm_row)


def _mlstm_kernel(q_ref, k_ref, v_ref, op_ref, gates_ref, cx0_ref, m0_ref, bias_ref, gn_ref,
                  x_ref, wo_ref, gffn_ref, wgu_ref, wd_ref, gfin_ref,
                  y_ref, cxout_ref, mout_ref, cx_scr, m_scr, oprev_scr, *, L, BB):
    H, DK, DV = ML_HEADS, ML_DK, ML_DV
    t = pl.program_id(1)
    last_chunk = pl.num_programs(1) - 2

    @pl.when(t == 0)
    def _():
        cx_scr[...] = cx0_ref[...]
        m_scr[...] = m0_ref[...]
        oprev_scr[...] = jnp.zeros_like(oprev_scr)

    causal, _ = _masks(L)
    tril = jnp.where(causal, 1.0, 0.0).astype(BF16)
    ones_v = jnp.ones((L, DV), BF16)
    mean_mat = jnp.full((DV, DV), 1.0 / DV, BF16)
    heads = range(H)

    def sequence_stages(b):
        pre = gates_ref[b] + bias_ref[...]
        bcum = _dot_sel(tril, -_softplus(-pre))
        d_all = pre - pltpu.roll(bcum, LANES - H, 1)
        d_t = d_all.T
        q = [q_ref[b, :, h * DK:(h + 1) * DK].astype(BF16) for h in heads]
        k = [k_ref[b, :, h * DK:(h + 1) * DK] * (DK ** -0.5) for h in heads]
        vx = [jnp.concatenate([v_ref[b, :, h * DV:(h + 1) * DV].astype(BF16), ones_v], axis=1)
              for h in heads]
        cx = [cx_scr[b, h] for h in heads]
        qk = [_dot(q[h], k[h], NT) for h in heads]
        qcx = [_dot(q[h], cx[h]) for h in heads]
        yield

        bcol = [_bcast_col(bcum, H + h, DV) for h in heads]
        dcol = [_bcast_col(d_all, h, DV) for h in heads]
        m_intra, s_hat = [], []
        for h in heads:
            log_d = jnp.where(causal, bcol[h][:, :L] + d_t[h:h + 1, :], NEG)
            mi = jnp.max(log_d, axis=-1, keepdims=True)
            m_intra.append(mi)
            s_hat.append(qk[h] * jnp.exp(log_d - mi))
        nhx = [_dot(s_hat[h], vx[h]) for h in heads]
        yield

        blast = [bcol[h][L - 1:L, :] for h in heads]
        m_w, kw = [], []
        for h in heads:
            log_w = blast[h] + dcol[h]
            mw = jnp.max(log_w, axis=0, keepdims=True)
            m_w.append(mw)
            kw.append(k[h] * jnp.exp(log_w - mw)[:, :DK])
        ckvx = [_dot(kw[h], vx[h], TN) for h in heads]
        yield

        m = [m_scr[b, h:h + 1, :] for h in heads]
        hcell = []
        for h in heads:
            log_inter = bcol[h] + m[h]
            m_t = jnp.maximum(log_inter, m_intra[h])
            s_inter = jnp.exp(log_inter - m_t)
            s_intra = jnp.exp(m_intra[h] - m_t)
            num = s_inter * qcx[h][:, :DV] + s_intra * nhx[h][:, :DV]
            qn = s_inter * qcx[h][:, DV:] + s_intra * nhx[h][:, DV:]
            hcell.append(num / jnp.maximum(jnp.abs(qn), jnp.exp(-m_t)))
        ms = [_dot(hcell[h] * hcell[h], mean_mat) for h in heads]
        yield

        for h in heads:
            m_new = jnp.maximum(blast[h] + m[h], m_w[h])
            a_old = jnp.exp(blast[h] + m[h] - m_new)
            a_new = jnp.exp(m_w[h] - m_new)
            cx_scr[b, h] = (jnp.concatenate([a_old, a_old], axis=1) * cx[h]
                            + jnp.concatenate([a_new, a_new], axis=1) * ckvx[h])
            m_scr[b, h:h + 1, :] = m_new
        yield

        for h in heads:
            oh = op_ref[b, :, h * DV:(h + 1) * DV]
            oprev_scr[b * L:(b + 1) * L, h * DV:(h + 1) * DV] = (
                hcell[h] * lax.rsqrt(ms[h] + RMS_EPS) * gn_ref[:, h * DV:(h + 1) * DV]
                * _sigmoid(oh)).astype(BF16)

    def ffn_stages():
        x = jnp.concatenate([x_ref[b] for b in range(BB)], axis=0)
        y = yield from _ffn_stages(x, oprev_scr[...], wo_ref, gffn_ref, wgu_ref, wd_ref, gfin_ref,
                                   final=True, yields=ML_STAGES)
        for b in range(BB):
            y_ref[b] = y[b * L:(b + 1) * L]

    _run_lockstep([ffn_stages()] + [sequence_stages(b) for b in range(BB)])

    @pl.when(t == last_chunk)
    def _():
        cxout_ref[...] = cx_scr[...]
        mout_ref[...] = m_scr[...]


def _mlstm_layer(q, k, v, o_pre, gates, cx0, m0x, bias_row, gnorm_row, x, w_out, g_ffn, w_gu, w_down,
                 g_final, B, T, L):
    nt = T // L
    H, DK, DV = ML_HEADS, ML_DK, ML_DV
    BB = _seqs_per_step(B, ML_SEQS_PER_STEP)
    tok = lambda b, t: (b, jnp.minimum(t, nt - 1), 0)
    prev_tok = lambda b, t: (b, jnp.maximum(t - 1, 0), 0)
    resident = pl.Buffered(1)
    per_seq3 = lambda b, t: (b, 0, 0)
    per_seq4 = lambda b, t: (b, 0, 0, 0)
    const2 = lambda b, t: (0, 0)
    return pl.pallas_call(
        functools.partial(_mlstm_kernel, L=L, BB=BB),
        grid=(B // BB, nt + 1),
        in_specs=[
            pl.BlockSpec((BB, L, ML_KW), tok),
            pl.BlockSpec((BB, L, ML_KW), tok),
            pl.BlockSpec((BB, L, ML_VW), tok),
            pl.BlockSpec((BB, L, ML_VW), tok),
            pl.BlockSpec((BB, L, LANES), tok),
            pl.BlockSpec((BB, H, DK, 2 * DV), per_seq4),
            pl.BlockSpec((BB, H, LANES), per_seq3),
            pl.BlockSpec((1, LANES), const2),
            pl.BlockSpec((1, ML_VW), const2),
            pl.BlockSpec((BB, L, D_MODEL), prev_tok),
            pl.BlockSpec(w_out.shape, const2, pipeline_mode=resident),
            pl.BlockSpec((1, D_MODEL), const2),
            pl.BlockSpec(w_gu.shape, const2, pipeline_mode=resident),
            pl.BlockSpec(w_down.shape, const2, pipeline_mode=resident),
            pl.BlockSpec((1, D_MODEL), const2),
        ],
        out_specs=[
            pl.BlockSpec((BB, L, D_MODEL), prev_tok),
            pl.BlockSpec((BB, H, DK, 2 * DV), per_seq4),
            pl.BlockSpec((BB, H, LANES), per_seq3),
        ],
        out_shape=[
            jax.ShapeDtypeStruct((B, T, D_MODEL), F32),
            jax.ShapeDtypeStruct((B, H, DK, 2 * DV), F32),
            jax.ShapeDtypeStruct((B, H, LANES), F32),
        ],
        scratch_shapes=[
            pltpu.VMEM((BB, H, DK, 2 * DV), F32),
            pltpu.VMEM((BB, H, LANES), F32),
            pltpu.VMEM((BB * L, ML_VW), BF16),
        ],
        compiler_params=pltpu.CompilerParams(
            dimension_semantics=("arbitrary", "arbitrary"), vmem_limit_bytes=VMEM_LIMIT),
        name="mlstm_layer",
    )(q, k, v, o_pre, gates, cx0, m0x, bias_row, gnorm_row, x, w_out, g_ffn, w_gu, w_down, g_final)


def _drain(stage_generator):
    try:
        while True:
            next(stage_generator)
    except StopIteration as done:
        return done.value


def _ffn_stages(x, o, wo_ref, g_ref, wgu_ref, wd_ref, gfin_ref, *, final, yields):
    x1 = x + jnp.dot(o, wo_ref[...], preferred_element_type=F32)
    hn = _rms(x1, g_ref[...]).astype(BF16)
    acc = x1
    n_chunks = FFN_HIDDEN // FFN_CHUNK
    for c in range(n_chunks):
        lo = c * FFN_CHUNK
        gate = jnp.dot(hn, wgu_ref[:, lo:lo + FFN_CHUNK], preferred_element_type=F32)
        up = jnp.dot(hn, wgu_ref[:, FFN_HIDDEN + lo:FFN_HIDDEN + lo + FFN_CHUNK],
                     preferred_element_type=F32)
        act = (gate * _sigmoid(gate) * up).astype(BF16)
        acc = acc + jnp.dot(act, wd_ref[lo:lo + FFN_CHUNK, :], preferred_element_type=F32)
        if (c + 1) * yields // n_chunks > c * yields // n_chunks:
            yield
    if final:
        acc = _rms(acc, gfin_ref[...])
    return acc


def _out_ffn_kernel(x_ref, o_ref, wo_ref, g_ref, wgu_ref, wd_ref, gfin_ref, y_ref, *, final):
    y_ref[...] = _drain(_ffn_stages(x_ref[...], o_ref[...], wo_ref, g_ref, wgu_ref, wd_ref, gfin_ref,
                                    final=final, yields=0))


def _out_ffn(x, o, w_out, g, w_gu, w_down, g_final, final):
    n, d = x.shape
    tm = min(FFN_TOKEN_TILE, n)
    assert n % tm == 0 and FFN_HIDDEN % FFN_CHUNK == 0 and o.dtype == BF16
    tile = lambda i: (i, 0)
    const = lambda i: (0, 0)
    resident = pl.Buffered(1)
    return pl.pallas_call(
        functools.partial(_out_ffn_kernel, final=final),
        grid=(n // tm,),
        in_specs=[
            pl.BlockSpec((tm, d), tile),
            pl.BlockSpec((tm, o.shape[1]), tile),
            pl.BlockSpec(w_out.shape, const, pipeline_mode=resident),
            pl.BlockSpec((1, d), const),
            pl.BlockSpec(w_gu.shape, const, pipeline_mode=resident),
            pl.BlockSpec(w_down.shape, const, pipeline_mode=resident),
            pl.BlockSpec((1, d), const),
        ],
        out_specs=pl.BlockSpec((tm, d), tile),
        out_shape=jax.ShapeDtypeStruct((n, d), F32),
        compiler_params=pltpu.CompilerParams(
            dimension_semantics=("arbitrary",), vmem_limit_bytes=VMEM_LIMIT),
        name="out_ffn",
    )(x, o, w_out, g, w_gu, w_down, g_final)


def _pad_lanes(row, offset=0):
    out = jnp.zeros((1, LANES), F32)
    return lax.dynamic_update_slice(out, row.astype(F32)[None, :], (0, offset))


def _prep_weights(norm_mix, gdn_w_in, gdn_conv_w, gdn_a_log, gdn_dt_bias, gdn_norm, gdn_w_out,
                  ml_w_in, ml_b_i, ml_b_f, ml_norm, ml_w_out, norm_ffn, ffn_w_gu, ffn_w_down,
                  norm_final):
    H = GDN_HEADS
    w0 = gdn_w_in[0]
    gate_pad = jnp.zeros((D_MODEL, LANES - 2 * H), F32)
    w1 = ml_w_in[0]
    p = {
        "g_mix0": norm_mix[0][None, :], "g_mix1": norm_mix[1][None, :],
        "g_ffn0": norm_ffn[0][None, :], "g_ffn1": norm_ffn[1][None, :],
        "g_final": norm_final[None, :],
        "gdn_w_qkv": w0[:, :GDN_QKV].astype(BF16),
        "gdn_w_z": w0[:, GDN_QKV:GDN_QKV + GDN_VW].astype(BF16),
        "gdn_w_gates": jnp.concatenate([w0[:, GDN_QKV + GDN_VW:], gate_pad], axis=1).astype(BF16),
        "gdn_conv_w": gdn_conv_w[0],
        "gdn_alog": _pad_lanes(gdn_a_log[0]), "gdn_dtb": _pad_lanes(gdn_dt_bias[0]),
        "gdn_norm": gdn_norm[0][None, :],
        "gdn_w_out": gdn_w_out[0].astype(BF16),
        "ml_w_q": w1[:, :ML_KW].astype(BF16),
        "ml_w_k": w1[:, ML_KW:2 * ML_KW].astype(BF16),
        "ml_w_v": w1[:, 2 * ML_KW:2 * ML_KW + ML_VW].astype(BF16),
        "ml_w_o": w1[:, 2 * ML_KW + ML_VW:2 * ML_KW + 2 * ML_VW].astype(BF16),
        "ml_w_gates": jnp.concatenate([w1[:, 2 * ML_KW + 2 * ML_VW:], gate_pad], axis=1).astype(BF16),
        "ml_bias": _pad_lanes(jnp.concatenate([ml_b_i[0], ml_b_f[0]])),
        "ml_norm": ml_norm[0][None, :],
        "ml_w_out": ml_w_out[0].astype(BF16),
        "ffn_w_gu0": ffn_w_gu[0].astype(BF16), "ffn_w_gu1": ffn_w_gu[1].astype(BF16),
        "ffn_w_down0": ffn_w_down[0].astype(BF16), "ffn_w_down1": ffn_w_down[1].astype(BF16),
    }
    return p


def _trunk(x, conv0, s0, c0, n0, m0, L, p):
    B, T, D = x.shape
    N = B * T
    x2 = x.reshape(N, D)
    seq = lambda a: a.reshape(B, T, a.shape[-1])

    qkv, z, gates = _norm_proj(x2, p["g_mix0"], [p["gdn_w_qkv"], p["gdn_w_z"], p["gdn_w_gates"]])
    prev8 = jnp.concatenate(
        [jnp.zeros((B, SUBLANES - (CONV_W - 1), GDN_QKV), F32), conv0.astype(F32)], axis=1)
    o, s_fin = _gdn_recurrence(seq(qkv), seq(z), seq(gates), prev8, s0, p["gdn_conv_w"],
                               p["gdn_alog"], p["gdn_dtb"], p["gdn_norm"], B, T, L)
    conv_fin = seq(qkv)[:, T - (CONV_W - 1):, :]
    x2 = _out_ffn(x2, o.reshape(N, GDN_VW), p["gdn_w_out"], p["g_ffn0"], p["ffn_w_gu0"],
                  p["ffn_w_down0"], p["g_final"], final=False)

    q, k, v, o_pre, gates = _norm_proj(
        x2, p["g_mix1"], [p["ml_w_q"], p["ml_w_k"], p["ml_w_v"], p["ml_w_o"], p["ml_w_gates"]])
    m0x = jnp.broadcast_to(m0[:, :, None], (B, ML_HEADS, LANES))
    cx0 = jnp.concatenate(
        [c0, jnp.broadcast_to(n0[:, :, :, None], (B, ML_HEADS, ML_DK, ML_DV))], axis=-1)
    y, cx_fin, m_fin = _mlstm_layer(seq(q), seq(k), seq(v), seq(o_pre), seq(gates), cx0, m0x,
                                    p["ml_bias"], p["ml_norm"], seq(x2), p["ml_w_out"], p["g_ffn1"],
                                    p["ffn_w_gu1"], p["ffn_w_down1"], p["g_final"], B, T, L)
    c_fin, n_fin = cx_fin[..., :ML_DV], cx_fin[..., ML_DV]
    return (y, conv_fin[None], s_fin[None], c_fin[None], n_fin[None],
            m_fin[:, :, 0][None])


def kernel(x_prompt, x_sample, state_gdn_conv, state_gdn_S, state_mlstm_C, state_mlstm_n,
           state_mlstm_m, norm_mix, gdn_w_in, gdn_conv_w, gdn_a_log, gdn_dt_bias, gdn_norm,
           gdn_w_out, ml_w_in, ml_b_i, ml_b_f, ml_norm, ml_w_out, norm_ffn, ffn_w_gu,
           ffn_w_down, norm_final):
    p = _prep_weights(norm_mix, gdn_w_in, gdn_conv_w, gdn_a_log, gdn_dt_bias, gdn_norm, gdn_w_out,
                      ml_w_in, ml_b_i, ml_b_f, ml_norm, ml_w_out, norm_ffn, ffn_w_gu, ffn_w_down,
                      norm_final)
    Bp = x_prompt.shape[0]
    prompt = _trunk(
        x_prompt,
        jnp.zeros((Bp, CONV_W - 1, GDN_QKV), F32),
        jnp.zeros((Bp, GDN_HEADS, GDN_DK, GDN_DV), F32),
        jnp.zeros((Bp, ML_HEADS, ML_DK, ML_DV), F32),
        jnp.zeros((Bp, ML_HEADS, ML_DK), F32),
        jnp.zeros((Bp, ML_HEADS), F32),
        PROMPT_CHUNK, p)
    sample = _trunk(
        x_sample, state_gdn_conv[0], state_gdn_S[0], state_mlstm_C[0], state_mlstm_n[0],
        state_mlstm_m[0], x_sample.shape[1], p)
    return (prompt[0], sample[0]) + prompt[1:] + sample[1:]
```

```python
import functools

import jax
import jax.numpy as jnp
from jax import lax
from jax.experimental import pallas as pl
from jax.experimental.pallas import tpu as pltpu

F32 = jnp.float32
BF16 = jnp.bfloat16

D_MODEL = 1024
PROMPT_CHUNK = 64
GDN_HEADS = 8
GDN_DK = 128
GDN_DV = 128
GDN_KW = GDN_HEADS * GDN_DK
GDN_VW = GDN_HEADS * GDN_DV
GDN_QKV = 2 * GDN_KW + GDN_VW
CONV_W = 4
ML_HEADS = 8
ML_DK = 64
ML_DV = 128
ML_KW = ML_HEADS * ML_DK
ML_VW = ML_HEADS * ML_DV
FFN_HIDDEN = 2816
RMS_EPS = 1e-6
L2_EPS = 1e-6
NEG = -1e30

LANES = 128
SUBLANES = 8
TOKEN_TILE = 1024
FFN_CHUNK = 256
GDN_SEQS_PER_STEP = 4
ML_SEQS_PER_STEP = 4
VMEM_LIMIT = 56 * 1024 * 1024

NN = ((1,), (0,))
NT = ((1,), (1,))
TN = ((0,), (0,))


def _dot(a, b, dims=NN):
    return lax.dot_general(a.astype(BF16), b.astype(BF16), (dims, ((), ())),
                           preferred_element_type=F32)


def _dot_sel(sel, x, dims=NN):
    hi = x.astype(BF16)
    r1 = x - hi.astype(F32)
    mid = r1.astype(BF16)
    lo = (r1 - mid.astype(F32)).astype(BF16)
    d = lambda p: lax.dot_general(sel, p, (dims, ((), ())), preferred_element_type=F32)
    return d(hi) + d(mid) + d(lo)


def _sigmoid(x):
    return 1.0 / (1.0 + jnp.exp(-x))


def _softplus(x):
    return jnp.maximum(x, 0.0) + jnp.log1p(jnp.exp(-jnp.abs(x)))


def _rms(x, g):
    return x * lax.rsqrt(jnp.mean(x * x, axis=-1, keepdims=True) + RMS_EPS) * g


def _masks(L):
    ri = lax.broadcasted_iota(jnp.int32, (L, L), 0)
    ci = lax.broadcasted_iota(jnp.int32, (L, L), 1)
    return ri >= ci, ri > ci


def _bcast_col(x, c, width):
    return jnp.broadcast_to(x[:, c:c + 1], (x.shape[0], width))


def _norm_proj_kernel(x_ref, g_ref, *refs, n_out):
    w_refs, o_refs = refs[:n_out], refs[n_out:]
    h = _rms(x_ref[...], g_ref[...]).astype(BF16)
    for w_ref, o_ref in zip(w_refs, o_refs):
        o_ref[...] = jnp.dot(h, w_ref[...], preferred_element_type=F32).astype(o_ref.dtype)


def _norm_proj(x, g, weights):
    n, d = x.shape
    tm = min(TOKEN_TILE, n)
    assert n % tm == 0
    in_specs = [pl.BlockSpec((tm, d), lambda i: (i, 0)),
                pl.BlockSpec((1, d), lambda i: (0, 0))]
    in_specs += [pl.BlockSpec((d, cols), functools.partial(lambda i, blk: (0, blk), blk=blk),
                              pipeline_mode=pl.Buffered(1)) for _, cols, blk in weights]
    out_specs = [pl.BlockSpec((tm, cols), lambda i: (i, 0)) for _, cols, _ in weights]
    out_shape = [jax.ShapeDtypeStruct((n, cols), F32) for _, cols, _ in weights]
    return pl.pallas_call(
        functools.partial(_norm_proj_kernel, n_out=len(weights)),
        grid=(n // tm,),
        in_specs=in_specs, out_specs=out_specs, out_shape=out_shape,
        compiler_params=pltpu.CompilerParams(
            dimension_semantics=("arbitrary",), vmem_limit_bytes=VMEM_LIMIT),
        name="norm_proj",
    )(x, g, *[w for w, _, _ in weights])


def _run_lockstep(stage_generators):
    active = list(stage_generators)
    while active:
        for g in list(active):
            try:
                next(g)
            except StopIteration:
                active.remove(g)


def _unit_lower_inverses(a_list, L):
    ri = lax.broadcasted_iota(jnp.int32, (L, L), 0)
    ci = lax.broadcasted_iota(jnp.int32, (L, L), 1)
    eye = jnp.where(ri == ci, 1.0, 0.0)
    ms = [-a for a in a_list]
    ps = [eye + m for m in ms]
    mb = [m.astype(BF16) for m in ms]
    ms = [_dot(m, m) for m in mb]
    span = 2
    while span < L:
        yield
        mb = [m.astype(BF16) for m in ms]
        pb = [p.astype(BF16) for p in ps]
        if 2 * span >= L:
            ps = [p + _dot(b, m) for p, b, m in zip(ps, pb, mb)]
        else:
            rs = [_dot(jnp.concatenate([b, m], axis=0), m) for b, m in zip(pb, mb)]
            ps = [p + r[:L] for p, r in zip(ps, rs)]
            ms = [r[L:] for r in rs]
        span *= 2
    return ps


def _gdn_kernel(qkv_ref, z_ref, gates_ref, prev_ref, s0_ref, convw_ref, alog_ref, dtb_ref, gn_ref,
                o_ref, sout_ref, tail_scr, s_scr, *, L, BB):
    H, DK, DV, KW = GDN_HEADS, GDN_DK, GDN_DV, GDN_KW
    t = pl.program_id(1)

    @pl.when(t == 0)
    def _():
        tail_scr[...] = prev_ref[...]
        s_scr[...] = s0_ref[...]

    causal, strict = _masks(L)
    tril = jnp.where(causal, 1.0, 0.0).astype(BF16)
    cw = convw_ref[...]
    gn = gn_ref[...]
    tail_row = lax.broadcasted_iota(jnp.int32, (SUBLANES, GDN_QKV), 0)
    sum_mat = jnp.ones((DK, DK), BF16)
    mean_mat = jnp.full((DV, DV), 1.0 / DV, BF16)
    heads = range(H)

    def sequence_stages(b):
        x = qkv_ref[b]
        tail = tail_scr[b]
        y = x * cw[CONV_W - 1:CONV_W]
        for s in range(1, CONV_W):
            xs = pltpu.roll(x, s, 0)
            head = jnp.where(tail_row < s, pltpu.roll(tail, s, 0), xs[:SUBLANES])
            xs = jnp.concatenate([head, xs[SUBLANES:]], axis=0)
            y = y + xs * cw[CONV_W - 1 - s:CONV_W - s]
        tail_scr[b] = x[L - SUBLANES:L]
        y = y * _sigmoid(y)
        gt = gates_ref[b]
        g_all = -jnp.exp(alog_ref[...]) * _softplus(gt + dtb_ref[...])
        beta_all = _sigmoid(gt)
        dec = _dot_sel(tril, g_all)
        dec_t = dec.T
        qr = [y[:, h * DK:(h + 1) * DK] for h in heads]
        kr = [y[:, KW + h * DK:KW + (h + 1) * DK] for h in heads]
        v = [y[:, 2 * KW + h * DV:2 * KW + (h + 1) * DV] for h in heads]
        ss = [_dot(jnp.concatenate([qr[h] * qr[h], kr[h] * kr[h]], axis=0), sum_mat) for h in heads]
        yield

        q = [qr[h] * lax.rsqrt(ss[h][:L] + L2_EPS) * (DK ** -0.5) for h in heads]
        k = [kr[h] * lax.rsqrt(ss[h][L:] + L2_EPS) for h in heads]
        qk = [_dot(jnp.concatenate([q[h], k[h]], axis=0), k[h], NT) for h in heads]
        dcol = [_bcast_col(dec, h, DK) for h in heads]
        bcol = [_bcast_col(beta_all, H + h, DK) for h in heads]
        gamma = []
        for h in heads:
            diff = dcol[h][:, :L] - dec_t[h:h + 1, :]
            gamma.append(jnp.where(causal, jnp.exp(jnp.where(causal, diff, 0.0)), 0.0))
        yield

        attn = [qk[h][:L] * gamma[h] for h in heads]
        a = [jnp.where(strict, bcol[h][:, :L] * qk[h][L:] * gamma[h], 0.0) for h in heads]
        tm = yield from _unit_lower_inverses(a, L)
        edec = [jnp.exp(dcol[h]) for h in heads]
        uw = [_dot(tm[h], jnp.concatenate([v[h] * bcol[h], k[h] * (bcol[h] * edec[h])], axis=1))
              for h in heads]
        dlast = [dcol[h][L - 1:L, :] for h in heads]
        k_tail = [k[h] * jnp.exp(dlast[h] - dcol[h]) for h in heads]
        q_dec = [q[h] * edec[h] for h in heads]
        yield

        s = [s_scr[b, h] for h in heads]
        wq = [_dot(jnp.concatenate([uw[h][:, DV:], q_dec[h]], axis=0), s[h]) for h in heads]
        yield

        v_new = [uw[h][:, :DV] - wq[h][:L] for h in heads]
        o_intra = [_dot(attn[h], v_new[h]) for h in heads]
        ds = [_dot(k_tail[h], v_new[h], TN) for h in heads]
        yield

        for h in heads:
            s_scr[b, h] = s[h] * jnp.exp(dlast[h]) + ds[h]
        o = [wq[h][L:] + o_intra[h] for h in heads]
        ms = [_dot(o[h] * o[h], mean_mat) for h in heads]
        yield

        for h in heads:
            zh = z_ref[b, :, h * DV:(h + 1) * DV]
            o_ref[b, :, h * DV:(h + 1) * DV] = (
                o[h] * lax.rsqrt(ms[h] + RMS_EPS) * gn * (zh * _sigmoid(zh))).astype(o_ref.dtype)

    _run_lockstep(sequence_stages(b) for b in range(BB))

    @pl.when(t == pl.num_programs(1) - 1)
    def _():
        sout_ref[...] = s_scr[...]


def _seqs_per_step(B, want):
    bb = min(B, want)
    assert B % bb == 0
    return bb


def _gdn_recurrence(qkv, z, gates, prev8, s0, conv_w, alog_row, dtb_row, gnorm_row, B, T, L):
    nt = T // L
    H, DK, DV = GDN_HEADS, GDN_DK, GDN_DV
    BB = _seqs_per_step(B, GDN_SEQS_PER_STEP)
    tok = lambda b, t: (b, t, 0)
    per_seq3 = lambda b, t: (b, 0, 0)
    per_seq4 = lambda b, t: (b, 0, 0, 0)
    const2 = lambda b, t: (0, 0)
    return pl.pallas_call(
        functools.partial(_gdn_kernel, L=L, BB=BB),
        grid=(B // BB, nt),
        in_specs=[
            pl.BlockSpec((BB, L, GDN_QKV), tok),
            pl.BlockSpec((BB, L, GDN_VW), tok),
            pl.BlockSpec((BB, L, LANES), tok),
            pl.BlockSpec((BB, SUBLANES, GDN_QKV), per_seq3),
            pl.BlockSpec((BB, H, DK, DV), per_seq4),
            pl.BlockSpec((CONV_W, GDN_QKV), const2),
            pl.BlockSpec((1, LANES), const2),
            pl.BlockSpec((1, LANES), const2),
            pl.BlockSpec((1, DV), const2),
        ],
        out_specs=[
            pl.BlockSpec((BB, L, GDN_VW), tok),
            pl.BlockSpec((BB, H, DK, DV), per_seq4),
        ],
        out_shape=[
            jax.ShapeDtypeStruct((B, T, GDN_VW), BF16),
            jax.ShapeDtypeStruct((B, H, DK, DV), F32),
        ],
        scratch_shapes=[
            pltpu.VMEM((BB, SUBLANES, GDN_QKV), F32),
            pltpu.VMEM((BB, H, DK, DV), F32),
        ],
        compiler_params=pltpu.CompilerParams(
            dimension_semantics=("arbitrary", "arbitrary"), vmem_limit_bytes=VMEM_LIMIT),
        name="gdn_recurrence",
    )(qkv, z, gates, prev8, s0, conv_w, alog_row, dtb_row, gnorm_row)


def _mlstm_kernel(q_ref, k_ref, v_ref, op_ref, gates_ref, cx0_ref, m0_ref, bias_ref, gn_ref,
                  o_ref, cxout_ref, mout_ref, cx_scr, m_scr, *, L, BB):
    H, DK, DV = ML_HEADS, ML_DK, ML_DV
    t = pl.program_id(1)

    @pl.when(t == 0)
    def _():
        cx_scr[...] = cx0_ref[...]
        m_scr[...] = m0_ref[...]

    causal, _ = _masks(L)
    tril = jnp.where(causal, 1.0, 0.0).astype(BF16)
    ones_v = jnp.ones((L, DV), BF16)
    mean_mat = jnp.full((DV, DV), 1.0 / DV, BF16)
    heads = range(H)

    def sequence_stages(b):
        pre = gates_ref[b] + bias_ref[...]
        bcum = _dot_sel(tril, -_softplus(-pre))
        d_all = pre - pltpu.roll(bcum, LANES - H, 1)
        d_t = d_all.T
        q = [q_ref[b, :, h * DK:(h + 1) * DK].astype(BF16) for h in heads]
        k = [k_ref[b, :, h * DK:(h + 1) * DK] * (DK ** -0.5) for h in heads]
        vx = [jnp.concatenate([v_ref[b, :, h * DV:(h + 1) * DV].astype(BF16), ones_v], axis=1)
              for h in heads]
        cx = [cx_scr[b, h] for h in heads]
        qk = [_dot(q[h], k[h], NT) for h in heads]
        qcx = [_dot(q[h], cx[h]) for h in heads]
        yield

        bcol = [_bcast_col(bcum, H + h, DV) for h in heads]
        dcol = [_bcast_col(d_all, h, DV) for h in heads]
        m_intra, s_hat = [], []
        for h in heads:
            log_d = jnp.where(causal, bcol[h][:, :L] + d_t[h:h + 1, :], NEG)
            mi = jnp.max(log_d, axis=-1, keepdims=True)
            m_intra.append(mi)
            s_hat.append(qk[h] * jnp.exp(log_d - mi))
        nhx = [_dot(s_hat[h], vx[h]) for h in heads]
        yield

        blast = [bcol[h][L - 1:L, :] for h in heads]
        m_w, kw = [], []
        for h in heads:
            log_w = blast[h] + dcol[h]
            mw = jnp.max(log_w, axis=0, keepdims=True)
            m_w.append(mw)
            kw.append(k[h] * jnp.exp(log_w - mw)[:, :DK])
        ckvx = [_dot(kw[h], vx[h], TN) for h in heads]
        yield

        m = [m_scr[b, h:h + 1, :] for h in heads]
        hcell = []
        for h in heads:
            log_inter = bcol[h] + m[h]
            m_t = jnp.maximum(log_inter, m_intra[h])
            s_inter = jnp.exp(log_inter - m_t)
            s_intra = jnp.exp(m_intra[h] - m_t)
            num = s_inter * qcx[h][:, :DV] + s_intra * nhx[h][:, :DV]
            qn = s_inter * qcx[h][:, DV:] + s_intra * nhx[h][:, DV:]
            hcell.append(num / jnp.maximum(jnp.abs(qn), jnp.exp(-m_t)))
        ms = [_dot(hcell[h] * hcell[h], mean_mat) for h in heads]
        yield

        for h in heads:
            m_new = jnp.maximum(blast[h] + m[h], m_w[h])
            a_old = jnp.exp(blast[h] + m[h] - m_new)
            a_new = jnp.exp(m_w[h] - m_new)
            cx_scr[b, h] = (jnp.concatenate([a_old, a_old], axis=1) * cx[h]
                            + jnp.concatenate([a_new, a_new], axis=1) * ckvx[h])
            m_scr[b, h:h + 1, :] = m_new
        yield

        for h in heads:
            oh = op_ref[b, :, h * DV:(h + 1) * DV]
            o_ref[b, :, h * DV:(h + 1) * DV] = (
                hcell[h] * lax.rsqrt(ms[h] + RMS_EPS) * gn_ref[:, h * DV:(h + 1) * DV]
                * _sigmoid(oh)).astype(o_ref.dtype)

    _run_lockstep(sequence_stages(b) for b in range(BB))

    @pl.when(t == pl.num_programs(1) - 1)
    def _():
        cxout_ref[...] = cx_scr[...]
        mout_ref[...] = m_scr[...]


def _mlstm_recurrence(q, k, v, o_pre, gates, cx0, m0x, bias_row, gnorm_row, B, T, L):
    nt = T // L
    H, DK, DV = ML_HEADS, ML_DK, ML_DV
    BB = _seqs_per_step(B, ML_SEQS_PER_STEP)
    tok = lambda b, t: (b, t, 0)
    per_seq3 = lambda b, t: (b, 0, 0)
    per_seq4 = lambda b, t: (b, 0, 0, 0)
    const2 = lambda b, t: (0, 0)
    return pl.pallas_call(
        functools.partial(_mlstm_kernel, L=L, BB=BB),
        grid=(B // BB, nt),
        in_specs=[
            pl.BlockSpec((BB, L, ML_KW), tok),
            pl.BlockSpec((BB, L, ML_KW), tok),
            pl.BlockSpec((BB, L, ML_VW), tok),
            pl.BlockSpec((BB, L, ML_VW), tok),
            pl.BlockSpec((BB, L, LANES), tok),
            pl.BlockSpec((BB, H, DK, 2 * DV), per_seq4),
            pl.BlockSpec((BB, H, LANES), per_seq3),
            pl.BlockSpec((1, LANES), const2),
            pl.BlockSpec((1, ML_VW), const2),
        ],
        out_specs=[
            pl.BlockSpec((BB, L, ML_VW), tok),
            pl.BlockSpec((BB, H, DK, 2 * DV), per_seq4),
            pl.BlockSpec((BB, H, LANES), per_seq3),
        ],
        out_shape=[
            jax.ShapeDtypeStruct((B, T, ML_VW), BF16),
            jax.ShapeDtypeStruct((B, H, DK, 2 * DV), F32),
            jax.ShapeDtypeStruct((B, H, LANES), F32),
        ],
        scratch_shapes=[
            pltpu.VMEM((BB, H, DK, 2 * DV), F32),
            pltpu.VMEM((BB, H, LANES), F32),
        ],
        compiler_params=pltpu.CompilerParams(
            dimension_semantics=("arbitrary", "arbitrary"), vmem_limit_bytes=VMEM_LIMIT),
        name="mlstm_recurrence",
    )(q, k, v, o_pre, gates, cx0, m0x, bias_row, gnorm_row)


def _out_ffn_kernel(x_ref, o_ref, wo_ref, g_ref, wgu_ref, wd_ref, gfin_ref, y_ref, *, final):
    x1 = x_ref[...] + jnp.dot(o_ref[...], wo_ref[...], preferred_element_type=F32)
    hn = _rms(x1, g_ref[...]).astype(BF16)
    acc = x1
    for c in range(FFN_HIDDEN // FFN_CHUNK):
        lo = c * FFN_CHUNK
        gate = jnp.dot(hn, wgu_ref[:, lo:lo + FFN_CHUNK], preferred_element_type=F32)
        up = jnp.dot(hn, wgu_ref[:, FFN_HIDDEN + lo:FFN_HIDDEN + lo + FFN_CHUNK],
                     preferred_element_type=F32)
        act = (gate * _sigmoid(gate) * up).astype(BF16)
        acc = acc + jnp.dot(act, wd_ref[lo:lo + FFN_CHUNK, :], preferred_element_type=F32)
    if final:
        acc = _rms(acc, gfin_ref[...])
    y_ref[...] = acc


def _out_ffn(x, o, w_out, g, w_gu, w_down, layer, g_final, final):
    n, d = x.shape
    tm = min(TOKEN_TILE, n)
    assert n % tm == 0 and FFN_HIDDEN % FFN_CHUNK == 0 and o.dtype == BF16
    tile = lambda i: (i, 0)
    const = lambda i: (0, 0)
    this_layer = lambda i: (layer, 0, 0)
    resident = pl.Buffered(1)
    return pl.pallas_call(
        functools.partial(_out_ffn_kernel, final=final),
        grid=(n // tm,),
        in_specs=[
            pl.BlockSpec((tm, d), tile),
            pl.BlockSpec((tm, o.shape[1]), tile),
            pl.BlockSpec(w_out.shape, const, pipeline_mode=resident),
            pl.BlockSpec((1, d), const),
            pl.BlockSpec((None,) + w_gu.shape[1:], this_layer, pipeline_mode=resident),
            pl.BlockSpec((None,) + w_down.shape[1:], this_layer, pipeline_mode=resident),
            pl.BlockSpec((1, d), const),
        ],
        out_specs=pl.BlockSpec((tm, d), tile),
        out_shape=jax.ShapeDtypeStruct((n, d), F32),
        compiler_params=pltpu.CompilerParams(
            dimension_semantics=("arbitrary",), vmem_limit_bytes=VMEM_LIMIT),
        name="out_ffn",
    )(x, o, w_out, g, w_gu, w_down, g_final)


def _pad_lanes(row, offset=0):
    out = jnp.zeros((1, LANES), F32)
    return lax.dynamic_update_slice(out, row.astype(F32)[None, :], (0, offset))


def _prep_weights(norm_mix, gdn_w_in, gdn_conv_w, gdn_a_log, gdn_dt_bias, gdn_norm, gdn_w_out,
                  ml_w_in, ml_b_i, ml_b_f, ml_norm, ml_w_out, norm_ffn, ffn_w_gu, ffn_w_down,
                  norm_final):
    H = GDN_HEADS
    w0 = gdn_w_in[0]
    gate_pad = jnp.zeros((D_MODEL, LANES - 2 * H), F32)
    w1 = ml_w_in[0]
    p = {
        "g_mix0": norm_mix[0][None, :], "g_mix1": norm_mix[1][None, :],
        "g_ffn0": norm_ffn[0][None, :], "g_ffn1": norm_ffn[1][None, :],
        "g_final": norm_final[None, :],
        "gdn_w_in": w0.astype(BF16),
        "gdn_w_gates": jnp.concatenate([w0[:, GDN_QKV + GDN_VW:], gate_pad], axis=1).astype(BF16),
        "gdn_conv_w": gdn_conv_w[0],
        "gdn_alog": _pad_lanes(gdn_a_log[0]), "gdn_dtb": _pad_lanes(gdn_dt_bias[0]),
        "gdn_norm": gdn_norm[0][None, :],
        "gdn_w_out": gdn_w_out[0].astype(BF16),
        "ml_w_in": w1.astype(BF16),
        "ml_w_gates": jnp.concatenate([w1[:, 2 * ML_KW + 2 * ML_VW:], gate_pad], axis=1).astype(BF16),
        "ml_bias": _pad_lanes(jnp.concatenate([ml_b_i[0], ml_b_f[0]])),
        "ml_norm": ml_norm[0][None, :],
        "ml_w_out": ml_w_out[0].astype(BF16),
        "ffn_w_gu": ffn_w_gu.astype(BF16), "ffn_w_down": ffn_w_down.astype(BF16),
    }
    return p


def _trunk(x, conv0, s0, c0, n0, m0, L, p):
    B, T, D = x.shape
    N = B * T
    x2 = x.reshape(N, D)
    seq = lambda a: a.reshape(B, T, a.shape[-1])

    qkv, z, gates = _norm_proj(x2, p["g_mix0"], [
        (p["gdn_w_in"], GDN_QKV, 0), (p["gdn_w_in"], GDN_VW, GDN_QKV // GDN_VW),
        (p["gdn_w_gates"], LANES, 0)])
    prev8 = jnp.concatenate(
        [jnp.zeros((B, SUBLANES - (CONV_W - 1), GDN_QKV), F32), conv0.astype(F32)], axis=1)
    o, s_fin = _gdn_recurrence(seq(qkv), seq(z), seq(gates), prev8, s0, p["gdn_conv_w"],
                               p["gdn_alog"], p["gdn_dtb"], p["gdn_norm"], B, T, L)
    conv_fin = seq(qkv)[:, T - (CONV_W - 1):, :]
    x2 = _out_ffn(x2, o.reshape(N, GDN_VW), p["gdn_w_out"], p["g_ffn0"], p["ffn_w_gu"],
                  p["ffn_w_down"], 0, p["g_final"], final=False)

    q, k, v, o_pre, gates = _norm_proj(x2, p["g_mix1"], [
        (p["ml_w_in"], ML_KW, 0), (p["ml_w_in"], ML_KW, 1),
        (p["ml_w_in"], ML_VW, 2 * ML_KW // ML_VW), (p["ml_w_in"], ML_VW, 2 * ML_KW // ML_VW + 1),
        (p["ml_w_gates"], LANES, 0)])
    m0x = jnp.broadcast_to(m0[:, :, None], (B, ML_HEADS, LANES))
    cx0 = jnp.concatenate(
        [c0, jnp.broadcast_to(n0[:, :, :, None], (B, ML_HEADS, ML_DK, ML_DV))], axis=-1)
    o, cx_fin, m_fin = _mlstm_recurrence(seq(q), seq(k), seq(v), seq(o_pre), seq(gates), cx0, m0x,
                                         p["ml_bias"], p["ml_norm"], B, T, L)
    c_fin, n_fin = cx_fin[..., :ML_DV], cx_fin[..., ML_DV]
    y = _out_ffn(x2, o.reshape(N, ML_VW), p["ml_w_out"], p["g_ffn1"], p["ffn_w_gu"],
                 p["ffn_w_down"], 1, p["g_final"], final=True)
    return (y.reshape(B, T, D), conv_fin[None], s_fin[None], c_fin[None], n_fin[None],
            m_fin[:, :, 0][None])


def kernel(x_prompt, x_sample, state_gdn_conv, state_gdn_S, state_mlstm_C, state_mlstm_n,
           state_mlstm_m, norm_mix, gdn_w_in, gdn_conv_w, gdn_a_log, gdn_dt_bias, gdn_norm,
           gdn_w_out, ml_w_in, ml_b_i, ml_b_f, ml_norm, ml_w_out, norm_ffn, ffn_w_gu,
           ffn_w_down, norm_final):
    p = _prep_weights(norm_mix, gdn_w_in, gdn_conv_w, gdn_a_log, gdn_dt_bias, gdn_norm, gdn_w_out,
                      ml_w_in, ml_b_i, ml_b_f, ml_norm, ml_w_out, norm_ffn, ffn_w_gu, ffn_w_down,
                      norm_final)
    Bp = x_prompt.shape[0]
    prompt = _trunk(
        x_prompt,
        jnp.zeros((Bp, CONV_W - 1, GDN_QKV), F32),
        jnp.zeros((Bp, GDN_HEADS, GDN_DK, GDN_DV), F32),
        jnp.zeros((Bp, ML_HEADS, ML_DK, ML_DV), F32),
        jnp.zeros((Bp, ML_HEADS, ML_DK), F32),
        jnp.zeros((Bp, ML_HEADS), F32),
        PROMPT_CHUNK, p)
    sample = _trunk(
        x_sample, state_gdn_conv[0], state_gdn_S[0], state_mlstm_C[0], state_mlstm_n[0],
        state_mlstm_m[0], x_sample.shape[1], p)
    return (prompt[0], sample[0]) + prompt[1:] + sample[1:]
```

```python
import functools

import jax
import jax.numpy as jnp
from jax import lax
from jax.experimental import pallas as pl
from jax.experimental.pallas import tpu as pltpu

F32 = jnp.float32
BF16 = jnp.bfloat16

D_MODEL = 1024
PROMPT_CHUNK = 64
GDN_HEADS = 8
GDN_DK = 128
GDN_DV = 128
GDN_KW = GDN_HEADS * GDN_DK
GDN_VW = GDN_HEADS * GDN_DV
GDN_QKV = 2 * GDN_KW + GDN_VW
CONV_W = 4
ML_HEADS = 8
ML_DK = 64
ML_DV = 128
ML_KW = ML_HEADS * ML_DK
ML_VW = ML_HEADS * ML_DV
FFN_HIDDEN = 2816
RMS_EPS = 1e-6
L2_EPS = 1e-6
NEG = -1e30

LANES = 128
SUBLANES = 8
TOKEN_TILE = 1024
FFN_CHUNK = 256
CONV_COLS = 512
GDN_SEQS_PER_STEP = 4
ML_SEQS_PER_STEP = 4
VMEM_LIMIT = 56 * 1024 * 1024

NN = ((1,), (0,))
NT = ((1,), (1,))
TN = ((0,), (0,))


def _dot(a, b, dims=NN):
    return lax.dot_general(a.astype(BF16), b.astype(BF16), (dims, ((), ())),
                           preferred_element_type=F32)


def _dot_sel(sel, x, dims=NN):
    hi = x.astype(BF16)
    r1 = x - hi.astype(F32)
    mid = r1.astype(BF16)
    lo = (r1 - mid.astype(F32)).astype(BF16)
    d = lambda p: lax.dot_general(sel, p, (dims, ((), ())), preferred_element_type=F32)
    return d(hi) + d(mid) + d(lo)


def _sigmoid(x):
    return 1.0 / (1.0 + jnp.exp(-x))


def _softplus(x):
    return jnp.maximum(x, 0.0) + jnp.log1p(jnp.exp(-jnp.abs(x)))


def _rms(x, g):
    return x * lax.rsqrt(jnp.mean(x * x, axis=-1, keepdims=True) + RMS_EPS) * g


def _masks(L):
    ri = lax.broadcasted_iota(jnp.int32, (L, L), 0)
    ci = lax.broadcasted_iota(jnp.int32, (L, L), 1)
    return ri >= ci, ri > ci


def _bcast_col(x, c, width):
    return jnp.broadcast_to(x[:, c:c + 1], (x.shape[0], width))


def _causal_conv_silu(x, tail, cw):
    tail_row = lax.broadcasted_iota(jnp.int32, tail.shape, 0)
    y = x * cw[CONV_W - 1:CONV_W]
    for s in range(1, CONV_W):
        xs = pltpu.roll(x, s, 0)
        head = jnp.where(tail_row < s, pltpu.roll(tail, s, 0), xs[:SUBLANES])
        xs = jnp.concatenate([head, xs[SUBLANES:]], axis=0)
        y = y + xs * cw[CONV_W - 1 - s:CONV_W - s]
    return y * _sigmoid(y)


def _norm_proj_kernel(x_ref, g_ref, *refs, n_out):
    w_refs, o_refs = refs[:n_out], refs[n_out:]
    h = _rms(x_ref[...], g_ref[...]).astype(BF16)
    for w_ref, o_ref in zip(w_refs, o_refs):
        o_ref[...] = jnp.dot(h, w_ref[...], preferred_element_type=F32).astype(o_ref.dtype)


def _norm_proj(x, g, weights):
    n, d = x.shape
    tm = min(TOKEN_TILE, n)
    assert n % tm == 0
    in_specs = [pl.BlockSpec((tm, d), lambda i: (i, 0)),
                pl.BlockSpec((1, d), lambda i: (0, 0))]
    in_specs += [pl.BlockSpec((d, cols), functools.partial(lambda i, blk: (0, blk), blk=blk),
                              pipeline_mode=pl.Buffered(1)) for _, cols, blk in weights]
    out_specs = [pl.BlockSpec((tm, cols), lambda i: (i, 0)) for _, cols, _ in weights]
    out_shape = [jax.ShapeDtypeStruct((n, cols), F32) for _, cols, _ in weights]
    return pl.pallas_call(
        functools.partial(_norm_proj_kernel, n_out=len(weights)),
        grid=(n // tm,),
        in_specs=in_specs, out_specs=out_specs, out_shape=out_shape,
        compiler_params=pltpu.CompilerParams(
            dimension_semantics=("arbitrary",), vmem_limit_bytes=VMEM_LIMIT),
        name="norm_proj",
    )(x, g, *[w for w, _, _ in weights])


def _norm_proj_conv_kernel(x_ref, g_ref, wqkv_ref, wz_ref, wg_ref, prev_ref, convw_ref,
                           y_ref, z_ref, gates_ref, rawtail_ref, tail_scr, *, tiles_per_seq):
    i = pl.program_id(0)

    @pl.when(i % tiles_per_seq == 0)
    def _():
        tail_scr[...] = prev_ref[...]

    tm = x_ref.shape[0]
    h = _rms(x_ref[...], g_ref[...]).astype(BF16)
    for j in range(GDN_QKV // CONV_COLS):
        cols = slice(j * CONV_COLS, (j + 1) * CONV_COLS)
        raw = jnp.dot(h, wqkv_ref[:, cols], preferred_element_type=F32)
        y_ref[:, cols] = _causal_conv_silu(raw, tail_scr[:, cols], convw_ref[:, cols])
        tail_scr[:, cols] = raw[tm - SUBLANES:tm]
        rawtail_ref[:, cols] = raw[tm - SUBLANES:tm]
    z_ref[...] = jnp.dot(h, wz_ref[...], preferred_element_type=F32)
    gates_ref[...] = jnp.dot(h, wg_ref[...], preferred_element_type=F32)


def _norm_proj_conv(x, g, w_in, w_gates, prev8, conv_w, T):
    n, d = x.shape
    tm = TOKEN_TILE
    assert T % tm == 0 and GDN_QKV % CONV_COLS == 0 and GDN_QKV % GDN_VW == 0
    tiles_per_seq = T // tm
    tile = lambda i: (i, 0)
    const = lambda i: (0, 0)
    resident = pl.Buffered(1)
    return pl.pallas_call(
        functools.partial(_norm_proj_conv_kernel, tiles_per_seq=tiles_per_seq),
        grid=(n // tm,),
        in_specs=[
            pl.BlockSpec((tm, d), tile),
            pl.BlockSpec((1, d), const),
            pl.BlockSpec((d, GDN_QKV), const, pipeline_mode=resident),
            pl.BlockSpec((d, GDN_VW), lambda i: (0, GDN_QKV // GDN_VW), pipeline_mode=resident),
            pl.BlockSpec((d, LANES), const, pipeline_mode=resident),
            pl.BlockSpec((None, SUBLANES, GDN_QKV), lambda i: (i // tiles_per_seq, 0, 0)),
            pl.BlockSpec((CONV_W, GDN_QKV), const),
        ],
        out_specs=[
            pl.BlockSpec((tm, GDN_QKV), tile),
            pl.BlockSpec((tm, GDN_VW), tile),
            pl.BlockSpec((tm, LANES), tile),
            pl.BlockSpec((None, SUBLANES, GDN_QKV), lambda i: (i, 0, 0)),
        ],
        out_shape=[
            jax.ShapeDtypeStruct((n, GDN_QKV), F32),
            jax.ShapeDtypeStruct((n, GDN_VW), F32),
            jax.ShapeDtypeStruct((n, LANES), F32),
            jax.ShapeDtypeStruct((n // tm, SUBLANES, GDN_QKV), F32),
        ],
        scratch_shapes=[pltpu.VMEM((SUBLANES, GDN_QKV), F32)],
        compiler_params=pltpu.CompilerParams(
            dimension_semantics=("arbitrary",), vmem_limit_bytes=VMEM_LIMIT),
        name="norm_proj_conv",
    )(x, g, w_in, w_in, w_gates, prev8, conv_w)


def _run_lockstep(stage_generators):
    active = list(stage_generators)
    while active:
        for g in list(active):
            try:
                next(g)
            except StopIteration:
                active.remove(g)


def _unit_lower_inverses(a_list, L):
    ri = lax.broadcasted_iota(jnp.int32, (L, L), 0)
    ci = lax.broadcasted_iota(jnp.int32, (L, L), 1)
    eye = jnp.where(ri == ci, 1.0, 0.0)
    ms = [-a for a in a_list]
    ps = [eye + m for m in ms]
    mb = [m.astype(BF16) for m in ms]
    ms = [_dot(m, m) for m in mb]
    span = 2
    while span < L:
        yield
        mb = [m.astype(BF16) for m in ms]
        pb = [p.astype(BF16) for p in ps]
        if 2 * span >= L:
            ps = [p + _dot(b, m) for p, b, m in zip(ps, pb, mb)]
        else:
            rs = [_dot(jnp.concatenate([b, m], axis=0), m) for b, m in zip(pb, mb)]
            ps = [p + r[:L] for p, r in zip(ps, rs)]
            ms = [r[L:] for r in rs]
        span *= 2
    return ps


def _gdn_kernel(qkv_ref, z_ref, gates_ref, prev_ref, s0_ref, convw_ref, alog_ref, dtb_ref, gn_ref,
                o_ref, sout_ref, tail_scr, s_scr, *, L, BB, conv_done):
    H, DK, DV, KW = GDN_HEADS, GDN_DK, GDN_DV, GDN_KW
    t = pl.program_id(1)

    @pl.when(t == 0)
    def _():
        tail_scr[...] = prev_ref[...]
        s_scr[...] = s0_ref[...]

    causal, strict = _masks(L)
    causal_f = jnp.where(causal, 1.0, 0.0)
    strict_f = jnp.where(strict, 1.0, 0.0)
    tril = causal_f.astype(BF16)
    cw = convw_ref[...]
    gn = gn_ref[...]
    sum_mat = jnp.ones((DK, DK), BF16)
    mean_mat = jnp.full((DV, DV), 1.0 / DV, BF16)
    heads = range(H)

    def sequence_stages(b):
        x = qkv_ref[b]
        if conv_done:
            y = x
        else:
            y = _causal_conv_silu(x, tail_scr[b], cw)
            tail_scr[b] = x[L - SUBLANES:L]
        gt = gates_ref[b]
        g_all = -jnp.exp(alog_ref[...]) * _softplus(gt + dtb_ref[...])
        beta_all = _sigmoid(gt)
        dec = _dot_sel(tril, g_all)
        dec_t = dec.T
        qr = [y[:, h * DK:(h + 1) * DK] for h in heads]
        kr = [y[:, KW + h * DK:KW + (h + 1) * DK] for h in heads]
        v = [y[:, 2 * KW + h * DV:2 * KW + (h + 1) * DV] for h in heads]
        ss = [_dot(jnp.concatenate([qr[h] * qr[h], kr[h] * kr[h]], axis=0), sum_mat) for h in heads]
        yield

        q = [qr[h] * lax.rsqrt(ss[h][:L] + L2_EPS) * (DK ** -0.5) for h in heads]
        k = [kr[h] * lax.rsqrt(ss[h][L:] + L2_EPS) for h in heads]
        qk = [_dot(jnp.concatenate([q[h], k[h]], axis=0), k[h], NT) for h in heads]
        dcol = [_bcast_col(dec, h, DK) for h in heads]
        bcol = [_bcast_col(beta_all, H + h, DK) for h in heads]
        gamma = []
        for h in heads:
            diff = dcol[h][:, :L] - dec_t[h:h + 1, :]
            gamma.append(jnp.exp(jnp.minimum(diff, 0.0)) * causal_f)
        yield

        attn = [qk[h][:L] * gamma[h] for h in heads]
        a = [bcol[h][:, :L] * qk[h][L:] * (gamma[h] * strict_f) for h in heads]
        tm = yield from _unit_lower_inverses(a, L)
        edec = [jnp.exp(dcol[h]) for h in heads]
        uw = [_dot(tm[h], jnp.concatenate([v[h] * bcol[h], k[h] * (bcol[h] * edec[h])], axis=1))
              for h in heads]
        dlast = [dcol[h][L - 1:L, :] for h in heads]
        k_tail = [k[h] * jnp.exp(dlast[h] - dcol[h]) for h in heads]
        q_dec = [q[h] * edec[h] for h in heads]
        yield

        s = [s_scr[b, h] for h in heads]
        wq = [_dot(jnp.concatenate([uw[h][:, DV:], q_dec[h]], axis=0), s[h]) for h in heads]
        yield

        v_new = [uw[h][:, :DV] - wq[h][:L] for h in heads]
        o_intra = [_dot(attn[h], v_new[h]) for h in heads]
        ds = [_dot(k_tail[h], v_new[h], TN) for h in heads]
        yield

        for h in heads:
            s_scr[b, h] = s[h] * jnp.exp(dlast[h]) + ds[h]
        o = [wq[h][L:] + o_intra[h] for h in heads]
        ms = [_dot(o[h] * o[h], mean_mat) for h in heads]
        yield

        for h in heads:
            zh = z_ref[b, :, h * DV:(h + 1) * DV]
            o_ref[b, :, h * DV:(h + 1) * DV] = (
                o[h] * lax.rsqrt(ms[h] + RMS_EPS) * gn * (zh * _sigmoid(zh))).astype(o_ref.dtype)

    _run_lockstep(sequence_stages(b) for b in range(BB))

    @pl.when(t == pl.num_programs(1) - 1)
    def _():
        sout_ref[...] = s_scr[...]


def _seqs_per_step(B, want):
    bb = min(B, want)
    assert B % bb == 0
    return bb


def _gdn_recurrence(qkv, z, gates, prev8, s0, conv_w, alog_row, dtb_row, gnorm_row, B, T, L, conv_done):
    nt = T // L
    H, DK, DV = GDN_HEADS, GDN_DK, GDN_DV
    BB = _seqs_per_step(B, GDN_SEQS_PER_STEP)
    tok = lambda b, t: (b, t, 0)
    per_seq3 = lambda b, t: (b, 0, 0)
    per_seq4 = lambda b, t: (b, 0, 0, 0)
    const2 = lambda b, t: (0, 0)
    return pl.pallas_call(
        functools.partial(_gdn_kernel, L=L, BB=BB, conv_done=conv_done),
        grid=(B // BB, nt),
        in_specs=[
            pl.BlockSpec((BB, L, GDN_QKV), tok),
            pl.BlockSpec((BB, L, GDN_VW), tok),
            pl.BlockSpec((BB, L, LANES), tok),
            pl.BlockSpec((BB, SUBLANES, GDN_QKV), per_seq3),
            pl.BlockSpec((BB, H, DK, DV), per_seq4),
            pl.BlockSpec((CONV_W, GDN_QKV), const2),
            pl.BlockSpec((1, LANES), const2),
            pl.BlockSpec((1, LANES), const2),
            pl.BlockSpec((1, DV), const2),
        ],
        out_specs=[
            pl.BlockSpec((BB, L, GDN_VW), tok),
            pl.BlockSpec((BB, H, DK, DV), per_seq4),
        ],
        out_shape=[
            jax.ShapeDtypeStruct((B, T, GDN_VW), BF16),
            jax.ShapeDtypeStruct((B, H, DK, DV), F32),
        ],
        scratch_shapes=[
            pltpu.VMEM((BB, SUBLANES, GDN_QKV), F32),
            pltpu.VMEM((BB, H, DK, DV), F32),
        ],
        compiler_params=pltpu.CompilerParams(
            dimension_semantics=("arbitrary", "arbitrary"), vmem_limit_bytes=VMEM_LIMIT),
        name="gdn_recurrence",
    )(qkv, z, gates, prev8, s0, conv_w, alog_row, dtb_row, gnorm_row)


def _mlstm_kernel(q_ref, k_ref, v_ref, op_ref, gates_ref, cx0_ref, m0_ref, bias_ref, gn_ref,
                  o_ref, cxout_ref, mout_ref, cx_scr, m_scr, *, L, BB):
    H, DK, DV = ML_HEADS, ML_DK, ML_DV
    t = pl.program_id(1)

    @pl.when(t == 0)
    def _():
        cx_scr[...] = cx0_ref[...]
        m_scr[...] = m0_ref[...]

    causal, _ = _masks(L)
    tril = jnp.where(causal, 1.0, 0.0).astype(BF16)
    ones_v = jnp.ones((L, DV), BF16)
    mean_mat = jnp.full((DV, DV), 1.0 / DV, BF16)
    heads = range(H)

    def sequence_stages(b):
        pre = gates_ref[b] + bias_ref[...]
        bcum = _dot_sel(tril, -_softplus(-pre))
        d_all = pre - pltpu.roll(bcum, LANES - H, 1)
        d_t = d_all.T
        q = [q_ref[b, :, h * DK:(h + 1) * DK].astype(BF16) for h in heads]
        k = [k_ref[b, :, h * DK:(h + 1) * DK] * (DK ** -0.5) for h in heads]
        vx = [jnp.concatenate([v_ref[b, :, h * DV:(h + 1) * DV].astype(BF16), ones_v], axis=1)
              for h in heads]
        cx = [cx_scr[b, h] for h in heads]
        qk = [_dot(q[h], k[h], NT) for h in heads]
        qcx = [_dot(q[h], cx[h]) for h in heads]
        yield

        bcol = [_bcast_col(bcum, H + h, DV) for h in heads]
        dcol = [_bcast_col(d_all, h, DV) for h in heads]
        m_intra, s_hat = [], []
        for h in heads:
            log_d = jnp.where(causal, bcol[h][:, :L] + d_t[h:h + 1, :], NEG)
            mi = jnp.max(log_d, axis=-1, keepdims=True)
            m_intra.append(mi)
            s_hat.append(qk[h] * jnp.exp(log_d - mi))
        nhx = [_dot(s_hat[h], vx[h]) for h in heads]
        yield

        blast = [bcol[h][L - 1:L, :] for h in heads]
        m_w, kw = [], []
        for h in heads:
            log_w = blast[h] + dcol[h]
            mw = jnp.max(log_w, axis=0, keepdims=True)
            m_w.append(mw)
            kw.append(k[h] * jnp.exp(log_w - mw)[:, :DK])
        ckvx = [_dot(kw[h], vx[h], TN) for h in heads]
        yield

        m = [m_scr[b, h:h + 1, :] for h in heads]
        hcell = []
        for h in heads:
            log_inter = bcol[h] + m[h]
            m_t = jnp.maximum(log_inter, m_intra[h])
            s_inter = jnp.exp(log_inter - m_t)
            s_intra = jnp.exp(m_intra[h] - m_t)
            num = s_inter * qcx[h][:, :DV] + s_intra * nhx[h][:, :DV]
            qn = s_inter * qcx[h][:, DV:] + s_intra * nhx[h][:, DV:]
            hcell.append(num / jnp.maximum(jnp.abs(qn), jnp.exp(-m_t)))
        ms = [_dot(hcell[h] * hcell[h], mean_mat) for h in heads]
        yield

        for h in heads:
            m_new = jnp.maximum(blast[h] + m[h], m_w[h])
            a_old = jnp.exp(blast[h] + m[h] - m_new)
            a_new = jnp.exp(m_w[h] - m_new)
            cx_scr[b, h] = (jnp.concatenate([a_old, a_old], axis=1) * cx[h]
                            + jnp.concatenate([a_new, a_new], axis=1) * ckvx[h])
            m_scr[b, h:h + 1, :] = m_new
        yield

        for h in heads:
            oh = op_ref[b, :, h * DV:(h + 1) * DV]
            o_ref[b, :, h * DV:(h + 1) * DV] = (
                hcell[h] * lax.rsqrt(ms[h] + RMS_EPS) * gn_ref[:, h * DV:(h + 1) * DV]
                * _sigmoid(oh)).astype(o_ref.dtype)

    _run_lockstep(sequence_stages(b) for b in range(BB))

    @pl.when(t == pl.num_programs(1) - 1)
    def _():
        cxout_ref[...] = cx_scr[...]
        mout_ref[...] = m_scr[...]


def _mlstm_recurrence(q, k, v, o_pre, gates, cx0, m0x, bias_row, gnorm_row, B, T, L):
    nt = T // L
    H, DK, DV = ML_HEADS, ML_DK, ML_DV
    BB = _seqs_per_step(B, ML_SEQS_PER_STEP)
    tok = lambda b, t: (b, t, 0)
    per_seq3 = lambda b, t: (b, 0, 0)
    per_seq4 = lambda b, t: (b, 0, 0, 0)
    const2 = lambda b, t: (0, 0)
    return pl.pallas_call(
        functools.partial(_mlstm_kernel, L=L, BB=BB),
        grid=(B // BB, nt),
        in_specs=[
            pl.BlockSpec((BB, L, ML_KW), tok),
            pl.BlockSpec((BB, L, ML_KW), tok),
            pl.BlockSpec((BB, L, ML_VW), tok),
            pl.BlockSpec((BB, L, ML_VW), tok),
            pl.BlockSpec((BB, L, LANES), tok),
            pl.BlockSpec((BB, H, DK, 2 * DV), per_seq4),
            pl.BlockSpec((BB, H, LANES), per_seq3),
            pl.BlockSpec((1, LANES), const2),
            pl.BlockSpec((1, ML_VW), const2),
        ],
        out_specs=[
            pl.BlockSpec((BB, L, ML_VW), tok),
            pl.BlockSpec((BB, H, DK, 2 * DV), per_seq4),
            pl.BlockSpec((BB, H, LANES), per_seq3),
        ],
        out_shape=[
            jax.ShapeDtypeStruct((B, T, ML_VW), BF16),
            jax.ShapeDtypeStruct((B, H, DK, 2 * DV), F32),
            jax.ShapeDtypeStruct((B, H, LANES), F32),
        ],
        scratch_shapes=[
            pltpu.VMEM((BB, H, DK, 2 * DV), F32),
            pltpu.VMEM((BB, H, LANES), F32),
        ],
        compiler_params=pltpu.CompilerParams(
            dimension_semantics=("arbitrary", "arbitrary"), vmem_limit_bytes=VMEM_LIMIT),
        name="mlstm_recurrence",
    )(q, k, v, o_pre, gates, cx0, m0x, bias_row, gnorm_row)


def _out_ffn_kernel(x_ref, o_ref, wo_ref, g_ref, wgu_ref, wd_ref, gfin_ref, y_ref, *, final):
    x1 = x_ref[...] + jnp.dot(o_ref[...], wo_ref[...], preferred_element_type=F32)
    hn = _rms(x1, g_ref[...]).astype(BF16)
    acc = x1
    for c in range(FFN_HIDDEN // FFN_CHUNK):
        lo = c * FFN_CHUNK
        gate = jnp.dot(hn, wgu_ref[:, lo:lo + FFN_CHUNK], preferred_element_type=F32)
        up = jnp.dot(hn, wgu_ref[:, FFN_HIDDEN + lo:FFN_HIDDEN + lo + FFN_CHUNK],
                     preferred_element_type=F32)
        act = (gate * _sigmoid(gate) * up).astype(BF16)
        acc = acc + jnp.dot(act, wd_ref[lo:lo + FFN_CHUNK, :], preferred_element_type=F32)
    if final:
        acc = _rms(acc, gfin_ref[...])
    y_ref[...] = acc


def _out_ffn(x, o, w_out, g, w_gu, w_down, layer, g_final, final):
    n, d = x.shape
    tm = min(TOKEN_TILE, n)
    assert n % tm == 0 and FFN_HIDDEN % FFN_CHUNK == 0 and o.dtype == BF16
    tile = lambda i: (i, 0)
    const = lambda i: (0, 0)
    this_layer = lambda i: (layer, 0, 0)
    resident = pl.Buffered(1)
    return pl.pallas_call(
        functools.partial(_out_ffn_kernel, final=final),
        grid=(n // tm,),
        in_specs=[
            pl.BlockSpec((tm, d), tile),
            pl.BlockSpec((tm, o.shape[1]), tile),
            pl.BlockSpec(w_out.shape, const, pipeline_mode=resident),
            pl.BlockSpec((1, d), const),
            pl.BlockSpec((None,) + w_gu.shape[1:], this_layer, pipeline_mode=resident),
            pl.BlockSpec((None,) + w_down.shape[1:], this_layer, pipeline_mode=resident),
            pl.BlockSpec((1, d), const),
        ],
        out_specs=pl.BlockSpec((tm, d), tile),
        out_shape=jax.ShapeDtypeStruct((n, d), F32),
        compiler_params=pltpu.CompilerParams(
            dimension_semantics=("arbitrary",), vmem_limit_bytes=VMEM_LIMIT),
        name="out_ffn",
    )(x, o, w_out, g, w_gu, w_down, g_final)


def _pad_lanes(row, offset=0):
    out = jnp.zeros((1, LANES), F32)
    return lax.dynamic_update_slice(out, row.astype(F32)[None, :], (0, offset))


def _prep_weights(norm_mix, gdn_w_in, gdn_conv_w, gdn_a_log, gdn_dt_bias, gdn_norm, gdn_w_out,
                  ml_w_in, ml_b_i, ml_b_f, ml_norm, ml_w_out, norm_ffn, ffn_w_gu, ffn_w_down,
                  norm_final):
    H = GDN_HEADS
    w0 = gdn_w_in[0]
    gate_pad = jnp.zeros((D_MODEL, LANES - 2 * H), F32)
    w1 = ml_w_in[0]
    p = {
        "g_mix0": norm_mix[0][None, :], "g_mix1": norm_mix[1][None, :],
        "g_ffn0": norm_ffn[0][None, :], "g_ffn1": norm_ffn[1][None, :],
        "g_final": norm_final[None, :],
        "gdn_w_in": w0.astype(BF16),
        "gdn_w_gates": jnp.concatenate([w0[:, GDN_QKV + GDN_VW:], gate_pad], axis=1).astype(BF16),
        "gdn_conv_w": gdn_conv_w[0],
        "gdn_alog": _pad_lanes(gdn_a_log[0]), "gdn_dtb": _pad_lanes(gdn_dt_bias[0]),
        "gdn_norm": gdn_norm[0][None, :],
        "gdn_w_out": gdn_w_out[0].astype(BF16),
        "ml_w_in": w1.astype(BF16),
        "ml_w_gates": jnp.concatenate([w1[:, 2 * ML_KW + 2 * ML_VW:], gate_pad], axis=1).astype(BF16),
        "ml_bias": _pad_lanes(jnp.concatenate([ml_b_i[0], ml_b_f[0]])),
        "ml_norm": ml_norm[0][None, :],
        "ml_w_out": ml_w_out[0].astype(BF16),
        "ffn_w_gu": ffn_w_gu.astype(BF16), "ffn_w_down": ffn_w_down.astype(BF16),
    }
    return p


def _trunk(x, conv0, s0, c0, n0, m0, L, p):
    B, T, D = x.shape
    N = B * T
    x2 = x.reshape(N, D)
    seq = lambda a: a.reshape(B, T, a.shape[-1])

    prev8 = jnp.concatenate(
        [jnp.zeros((B, SUBLANES - (CONV_W - 1), GDN_QKV), F32), conv0.astype(F32)], axis=1)
    conv_in_proj = T % TOKEN_TILE == 0
    if conv_in_proj:
        qkv, z, gates, raw_tail = _norm_proj_conv(x2, p["g_mix0"], p["gdn_w_in"], p["gdn_w_gates"],
                                                  prev8, p["gdn_conv_w"], T)
        conv_fin = raw_tail.reshape(B, T // TOKEN_TILE, SUBLANES, GDN_QKV)[
            :, -1, SUBLANES - (CONV_W - 1):, :]
    else:
        qkv, z, gates = _norm_proj(x2, p["g_mix0"], [
            (p["gdn_w_in"], GDN_QKV, 0), (p["gdn_w_in"], GDN_VW, GDN_QKV // GDN_VW),
            (p["gdn_w_gates"], LANES, 0)])
        conv_fin = seq(qkv)[:, T - (CONV_W - 1):, :]
    o, s_fin = _gdn_recurrence(seq(qkv), seq(z), seq(gates), prev8, s0, p["gdn_conv_w"],
                               p["gdn_alog"], p["gdn_dtb"], p["gdn_norm"], B, T, L, conv_in_proj)
    x2 = _out_ffn(x2, o.reshape(N, GDN_VW), p["gdn_w_out"], p["g_ffn0"], p["ffn_w_gu"],
                  p["ffn_w_down"], 0, p["g_final"], final=False)

    q, k, v, o_pre, gates = _norm_proj(x2, p["g_mix1"], [
        (p["ml_w_in"], ML_KW, 0), (p["ml_w_in"], ML_KW, 1),
        (p["ml_w_in"], ML_VW, 2 * ML_KW // ML_VW), (p["ml_w_in"], ML_VW, 2 * ML_KW // ML_VW + 1),
        (p["ml_w_gates"], LANES, 0)])
    m0x = jnp.broadcast_to(m0[:, :, None], (B, ML_HEADS, LANES))
    cx0 = jnp.concatenate(
        [c0, jnp.broadcast_to(n0[:, :, :, None], (B, ML_HEADS, ML_DK, ML_DV))], axis=-1)
    o, cx_fin, m_fin = _mlstm_recurrence(seq(q), seq(k), seq(v), seq(o_pre), seq(gates), cx0, m0x,
                                         p["ml_bias"], p["ml_norm"], B, T, L)
    c_fin, n_fin = cx_fin[..., :ML_DV], cx_fin[..., ML_DV]
    y = _out_ffn(x2, o.reshape(N, ML_VW), p["ml_w_out"], p["g_ffn1"], p["ffn_w_gu"],
                 p["ffn_w_down"], 1, p["g_final"], final=True)
    return (y.reshape(B, T, D), conv_fin[None], s_fin[None], c_fin[None], n_fin[None],
            m_fin[:, :, 0][None])


def kernel(x_prompt, x_sample, state_gdn_conv, state_gdn_S, state_mlstm_C, state_mlstm_n,
           state_mlstm_m, norm_mix, gdn_w_in, gdn_conv_w, gdn_a_log, gdn_dt_bias, gdn_norm,
           gdn_w_out, ml_w_in, ml_b_i, ml_b_f, ml_norm, ml_w_out, norm_ffn, ffn_w_gu,
           ffn_w_down, norm_final):
    p = _prep_weights(norm_mix, gdn_w_in, gdn_conv_w, gdn_a_log, gdn_dt_bias, gdn_norm, gdn_w_out,
                      ml_w_in, ml_b_i, ml_b_f, ml_norm, ml_w_out, norm_ffn, ffn_w_gu, ffn_w_down,
                      norm_final)
    Bp = x_prompt.shape[0]
    prompt = _trunk(
        x_prompt,
        jnp.zeros((Bp, CONV_W - 1, GDN_QKV), F32),
        jnp.zeros((Bp, GDN_HEADS, GDN_DK, GDN_DV), F32),
        jnp.zeros((Bp, ML_HEADS, ML_DK, ML_DV), F32),
        jnp.zeros((Bp, ML_HEADS, ML_DK), F32),
        jnp.zeros((Bp, ML_HEADS), F32),
        PROMPT_CHUNK, p)
    sample = _trunk(
        x_sample, state_gdn_conv[0], state_gdn_S[0], state_mlstm_C[0], state_mlstm_n[0],
        state_mlstm_m[0], x_sample.shape[1], p)
    return (prompt[0], sample[0]) + prompt[1:] + sample[1:]
```

```python
import functools

import jax
import jax.numpy as jnp
from jax import lax
from jax.experimental import pallas as pl
from jax.experimental.pallas import tpu as pltpu

F32 = jnp.float32
BF16 = jnp.bfloat16

D_MODEL = 1024
PROMPT_CHUNK = 64
GDN_HEADS = 8
GDN_DK = 128
GDN_DV = 128
GDN_KW = GDN_HEADS * GDN_DK
GDN_VW = GDN_HEADS * GDN_DV
GDN_QKV = 2 * GDN_KW + GDN_VW
CONV_W = 4
ML_HEADS = 8
ML_DK = 64
ML_DV = 128
ML_KW = ML_HEADS * ML_DK
ML_VW = ML_HEADS * ML_DV
FFN_HIDDEN = 2816
RMS_EPS = 1e-6
L2_EPS = 1e-6
NEG = -1e30

LANES = 128
SUBLANES = 8
TOKEN_TILE = 1024
FFN_CHUNK = 256
CONV_COLS = 512
GDN_SEQS_PER_STEP = 4
ML_SEQS_PER_STEP = 4
VMEM_LIMIT = 56 * 1024 * 1024

NN = ((1,), (0,))
NT = ((1,), (1,))
TN = ((0,), (0,))


def _dot(a, b, dims=NN):
    return lax.dot_general(a.astype(BF16), b.astype(BF16), (dims, ((), ())),
                           preferred_element_type=F32)


def _dot_sel(sel, x, dims=NN):
    hi = x.astype(BF16)
    r1 = x - hi.astype(F32)
    mid = r1.astype(BF16)
    lo = (r1 - mid.astype(F32)).astype(BF16)
    d = lambda p: lax.dot_general(sel, p, (dims, ((), ())), preferred_element_type=F32)
    return d(hi) + d(mid) + d(lo)


def _sigmoid(x):
    return 1.0 / (1.0 + jnp.exp(-x))


def _softplus(x):
    return jnp.maximum(x, 0.0) + jnp.log1p(jnp.exp(-jnp.abs(x)))


def _rms(x, g):
    return x * lax.rsqrt(jnp.mean(x * x, axis=-1, keepdims=True) + RMS_EPS) * g


def _masks(L):
    ri = lax.broadcasted_iota(jnp.int32, (L, L), 0)
    ci = lax.broadcasted_iota(jnp.int32, (L, L), 1)
    return ri >= ci, ri > ci


def _bcast_col(x, c, width):
    return jnp.broadcast_to(x[:, c:c + 1], (x.shape[0], width))


def _causal_conv(x, tail, cw):
    tail_row = lax.broadcasted_iota(jnp.int32, tail.shape, 0)
    y = x * cw[CONV_W - 1:CONV_W]
    for s in range(1, CONV_W):
        xs = pltpu.roll(x, s, 0)
        head = jnp.where(tail_row < s, pltpu.roll(tail, s, 0), xs[:SUBLANES])
        xs = jnp.concatenate([head, xs[SUBLANES:]], axis=0)
        y = y + xs * cw[CONV_W - 1 - s:CONV_W - s]
    return y


def _norm_proj_kernel(x_ref, g_ref, *refs, n_out):
    w_refs, o_refs = refs[:n_out], refs[n_out:]
    h = _rms(x_ref[...], g_ref[...]).astype(BF16)
    for w_ref, o_ref in zip(w_refs, o_refs):
        o_ref[...] = jnp.dot(h, w_ref[...], preferred_element_type=F32).astype(o_ref.dtype)


def _norm_proj(x, g, weights):
    n, d = x.shape
    tm = min(TOKEN_TILE, n)
    assert n % tm == 0
    in_specs = [pl.BlockSpec((tm, d), lambda i: (i, 0)),
                pl.BlockSpec((1, d), lambda i: (0, 0))]
    in_specs += [pl.BlockSpec((d, cols), functools.partial(lambda i, blk: (0, blk), blk=blk),
                              pipeline_mode=pl.Buffered(1)) for _, cols, blk in weights]
    out_specs = [pl.BlockSpec((tm, cols), lambda i: (i, 0)) for _, cols, _ in weights]
    out_shape = [jax.ShapeDtypeStruct((n, cols), F32) for _, cols, _ in weights]
    return pl.pallas_call(
        functools.partial(_norm_proj_kernel, n_out=len(weights)),
        grid=(n // tm,),
        in_specs=in_specs, out_specs=out_specs, out_shape=out_shape,
        compiler_params=pltpu.CompilerParams(
            dimension_semantics=("arbitrary",), vmem_limit_bytes=VMEM_LIMIT),
        name="norm_proj",
    )(x, g, *[w for w, _, _ in weights])


def _norm_proj_conv_kernel(x_ref, g_ref, wqkv_ref, wz_ref, wg_ref, prev_ref, convw_ref,
                           y_ref, z_ref, gates_ref, rawtail_ref, tail_scr, *, tiles_per_seq):
    i = pl.program_id(0)

    @pl.when(i % tiles_per_seq == 0)
    def _():
        tail_scr[...] = prev_ref[...]

    tm = x_ref.shape[0]
    h = _rms(x_ref[...], g_ref[...]).astype(BF16)
    for j in range(GDN_QKV // CONV_COLS):
        cols = slice(j * CONV_COLS, (j + 1) * CONV_COLS)
        raw = jnp.dot(h, wqkv_ref[:, cols], preferred_element_type=F32)
        y_ref[:, cols] = _causal_conv(raw, tail_scr[:, cols], convw_ref[:, cols])
        tail_scr[:, cols] = raw[tm - SUBLANES:tm]
        rawtail_ref[:, cols] = raw[tm - SUBLANES:tm]
    z_ref[...] = jnp.dot(h, wz_ref[...], preferred_element_type=F32)
    gates_ref[...] = jnp.dot(h, wg_ref[...], preferred_element_type=F32)


def _norm_proj_conv(x, g, w_in, w_gates, prev8, conv_w, T):
    n, d = x.shape
    tm = TOKEN_TILE
    assert T % tm == 0 and GDN_QKV % CONV_COLS == 0 and GDN_QKV % GDN_VW == 0
    tiles_per_seq = T // tm
    tile = lambda i: (i, 0)
    const = lambda i: (0, 0)
    resident = pl.Buffered(1)
    return pl.pallas_call(
        functools.partial(_norm_proj_conv_kernel, tiles_per_seq=tiles_per_seq),
        grid=(n // tm,),
        in_specs=[
            pl.BlockSpec((tm, d), tile),
            pl.BlockSpec((1, d), const),
            pl.BlockSpec((d, GDN_QKV), const, pipeline_mode=resident),
            pl.BlockSpec((d, GDN_VW), lambda i: (0, GDN_QKV // GDN_VW), pipeline_mode=resident),
            pl.BlockSpec((d, LANES), const, pipeline_mode=resident),
            pl.BlockSpec((None, SUBLANES, GDN_QKV), lambda i: (i // tiles_per_seq, 0, 0)),
            pl.BlockSpec((CONV_W, GDN_QKV), const),
        ],
        out_specs=[
            pl.BlockSpec((tm, GDN_QKV), tile),
            pl.BlockSpec((tm, GDN_VW), tile),
            pl.BlockSpec((tm, LANES), tile),
            pl.BlockSpec((None, SUBLANES, GDN_QKV), lambda i: (i, 0, 0)),
        ],
        out_shape=[
            jax.ShapeDtypeStruct((n, GDN_QKV), F32),
            jax.ShapeDtypeStruct((n, GDN_VW), F32),
            jax.ShapeDtypeStruct((n, LANES), F32),
            jax.ShapeDtypeStruct((n // tm, SUBLANES, GDN_QKV), F32),
        ],
        scratch_shapes=[pltpu.VMEM((SUBLANES, GDN_QKV), F32)],
        compiler_params=pltpu.CompilerParams(
            dimension_semantics=("arbitrary",), vmem_limit_bytes=VMEM_LIMIT),
        name="norm_proj_conv",
    )(x, g, w_in, w_in, w_gates, prev8, conv_w)


def _run_lockstep(stage_generators):
    active = list(stage_generators)
    while active:
        for g in list(active):
            try:
                next(g)
            except StopIteration:
                active.remove(g)


def _unit_lower_inverses(a_list, L):
    ri = lax.broadcasted_iota(jnp.int32, (L, L), 0)
    ci = lax.broadcasted_iota(jnp.int32, (L, L), 1)
    eye = jnp.where(ri == ci, 1.0, 0.0)
    ms = [-a for a in a_list]
    ps = [eye + m for m in ms]
    mb = [m.astype(BF16) for m in ms]
    ms = [_dot(m, m) for m in mb]
    span = 2
    while span < L:
        yield
        mb = [m.astype(BF16) for m in ms]
        pb = [p.astype(BF16) for p in ps]
        if 2 * span >= L:
            ps = [p + _dot(b, m) for p, b, m in zip(ps, pb, mb)]
        else:
            rs = [_dot(jnp.concatenate([b, m], axis=0), m) for b, m in zip(pb, mb)]
            ps = [p + r[:L] for p, r in zip(ps, rs)]
            ms = [r[L:] for r in rs]
        span *= 2
    return ps


def _gdn_kernel(qkv_ref, z_ref, gates_ref, prev_ref, s0_ref, convw_ref, alog_ref, dtb_ref, gn_ref,
                o_ref, sout_ref, tail_scr, s_scr, *, L, BB, conv_done):
    H, DK, DV, KW = GDN_HEADS, GDN_DK, GDN_DV, GDN_KW
    t = pl.program_id(1)

    @pl.when(t == 0)
    def _():
        tail_scr[...] = prev_ref[...]
        s_scr[...] = s0_ref[...]

    causal, strict = _masks(L)
    causal_f = jnp.where(causal, 1.0, 0.0)
    strict_f = jnp.where(strict, 1.0, 0.0)
    tril = causal_f.astype(BF16)
    cw = convw_ref[...]
    gn = gn_ref[...]
    sum_mat = jnp.ones((DK, DK), BF16)
    mean_mat = jnp.full((DV, DV), 1.0 / DV, BF16)
    heads = range(H)

    def sequence_stages(b):
        x = qkv_ref[b]
        if not conv_done:
            conv = _causal_conv(x, tail_scr[b], cw)
            tail_scr[b] = x[L - SUBLANES:L]
            x = conv
        y = x * _sigmoid(x)
        gt = gates_ref[b]
        g_all = -jnp.exp(alog_ref[...]) * _softplus(gt + dtb_ref[...])
        beta_all = _sigmoid(gt)
        dec = _dot_sel(tril, g_all)
        dec_t = dec.T
        qr = [y[:, h * DK:(h + 1) * DK] for h in heads]
        kr = [y[:, KW + h * DK:KW + (h + 1) * DK] for h in heads]
        v = [y[:, 2 * KW + h * DV:2 * KW + (h + 1) * DV].astype(BF16) for h in heads]
        ss = [_dot(jnp.concatenate([qr[h] * qr[h], kr[h] * kr[h]], axis=0), sum_mat) for h in heads]
        yield

        q = [(qr[h] * lax.rsqrt(ss[h][:L] + L2_EPS) * (DK ** -0.5)).astype(BF16) for h in heads]
        k = [(kr[h] * lax.rsqrt(ss[h][L:] + L2_EPS)).astype(BF16) for h in heads]
        qk = [_dot(jnp.concatenate([q[h], k[h]], axis=0), k[h], NT) for h in heads]
        dcol = [_bcast_col(dec, h, DK) for h in heads]
        bcol = [_bcast_col(beta_all, H + h, DK) for h in heads]
        gamma = []
        for h in heads:
            diff = dcol[h][:, :L] - dec_t[h:h + 1, :]
            gamma.append(jnp.exp(jnp.minimum(diff, 0.0)) * causal_f)
        yield

        attn = [qk[h][:L] * gamma[h] for h in heads]
        a = [bcol[h][:, :L] * qk[h][L:] * (gamma[h] * strict_f) for h in heads]
        tm = yield from _unit_lower_inverses(a, L)
        edec = [jnp.exp(dcol[h]) for h in heads]
        uw = [_dot(tm[h], jnp.concatenate([v[h] * bcol[h].astype(BF16),
                                           k[h] * (bcol[h] * edec[h]).astype(BF16)], axis=1))
              for h in heads]
        dlast = [dcol[h][L - 1:L, :] for h in heads]
        k_tail = [k[h] * jnp.exp(dlast[h] - dcol[h]).astype(BF16) for h in heads]
        q_dec = [q[h] * edec[h].astype(BF16) for h in heads]
        yield

        s = [s_scr[b, h] for h in heads]
        wq = [_dot(jnp.concatenate([uw[h][:, DV:].astype(BF16), q_dec[h]], axis=0), s[h]) for h in heads]
        yield

        v_new = [uw[h][:, :DV] - wq[h][:L] for h in heads]
        o_intra = [_dot(attn[h], v_new[h]) for h in heads]
        ds = [_dot(k_tail[h], v_new[h], TN) for h in heads]
        yield

        for h in heads:
            s_scr[b, h] = s[h] * jnp.exp(dlast[h]) + ds[h]
        o = [wq[h][L:] + o_intra[h] for h in heads]
        ms = [_dot(o[h] * o[h], mean_mat) for h in heads]
        yield

        for h in heads:
            zh = z_ref[b, :, h * DV:(h + 1) * DV]
            o_ref[b, :, h * DV:(h + 1) * DV] = (
                o[h] * lax.rsqrt(ms[h] + RMS_EPS) * gn * (zh * _sigmoid(zh))).astype(o_ref.dtype)

    _run_lockstep(sequence_stages(b) for b in range(BB))

    @pl.when(t == pl.num_programs(1) - 1)
    def _():
        sout_ref[...] = s_scr[...]


def _seqs_per_step(B, want):
    bb = min(B, want)
    assert B % bb == 0
    return bb


def _gdn_recurrence(qkv, z, gates, prev8, s0, conv_w, alog_row, dtb_row, gnorm_row, B, T, L, conv_done):
    nt = T // L
    H, DK, DV = GDN_HEADS, GDN_DK, GDN_DV
    BB = _seqs_per_step(B, GDN_SEQS_PER_STEP)
    tok = lambda b, t: (b, t, 0)
    per_seq3 = lambda b, t: (b, 0, 0)
    per_seq4 = lambda b, t: (b, 0, 0, 0)
    const2 = lambda b, t: (0, 0)
    return pl.pallas_call(
        functools.partial(_gdn_kernel, L=L, BB=BB, conv_done=conv_done),
        grid=(B // BB, nt),
        in_specs=[
            pl.BlockSpec((BB, L, GDN_QKV), tok),
            pl.BlockSpec((BB, L, GDN_VW), tok),
            pl.BlockSpec((BB, L, LANES), tok),
            pl.BlockSpec((BB, SUBLANES, GDN_QKV), per_seq3),
            pl.BlockSpec((BB, H, DK, DV), per_seq4),
            pl.BlockSpec((CONV_W, GDN_QKV), const2),
            pl.BlockSpec((1, LANES), const2),
            pl.BlockSpec((1, LANES), const2),
            pl.BlockSpec((1, DV), const2),
        ],
        out_specs=[
            pl.BlockSpec((BB, L, GDN_VW), tok),
            pl.BlockSpec((BB, H, DK, DV), per_seq4),
        ],
        out_shape=[
            jax.ShapeDtypeStruct((B, T, GDN_VW), BF16),
            jax.ShapeDtypeStruct((B, H, DK, DV), F32),
        ],
        scratch_shapes=[
            pltpu.VMEM((BB, SUBLANES, GDN_QKV), F32),
            pltpu.VMEM((BB, H, DK, DV), F32),
        ],
        compiler_params=pltpu.CompilerParams(
            dimension_semantics=("arbitrary", "arbitrary"), vmem_limit_bytes=VMEM_LIMIT),
        name="gdn_recurrence",
    )(qkv, z, gates, prev8, s0, conv_w, alog_row, dtb_row, gnorm_row)


def _mlstm_kernel(q_ref, k_ref, v_ref, op_ref, gates_ref, cx0_ref, m0_ref, bias_ref, gn_ref,
                  o_ref, cxout_ref, mout_ref, cx_scr, m_scr, *, L, BB):
    H, DK, DV = ML_HEADS, ML_DK, ML_DV
    t = pl.program_id(1)

    @pl.when(t == 0)
    def _():
        cx_scr[...] = cx0_ref[...]
        m_scr[...] = m0_ref[...]

    causal, _ = _masks(L)
    tril = jnp.where(causal, 1.0, 0.0).astype(BF16)
    ones_v = jnp.ones((L, DV), BF16)
    mean_mat = jnp.full((DV, DV), 1.0 / DV, BF16)
    heads = range(H)

    def sequence_stages(b):
        pre = gates_ref[b] + bias_ref[...]
        bcum = _dot_sel(tril, -_softplus(-pre))
        d_all = pre - pltpu.roll(bcum, LANES - H, 1)
        d_t = d_all.T
        q = [q_ref[b, :, h * DK:(h + 1) * DK].astype(BF16) for h in heads]
        k = [k_ref[b, :, h * DK:(h + 1) * DK] * (DK ** -0.5) for h in heads]
        vx = [jnp.concatenate([v_ref[b, :, h * DV:(h + 1) * DV].astype(BF16), ones_v], axis=1)
              for h in heads]
        cx = [cx_scr[b, h] for h in heads]
        qk = [_dot(q[h], k[h], NT) for h in heads]
        qcx = [_dot(q[h], cx[h]) for h in heads]
        yield

        bcol = [_bcast_col(bcum, H + h, DV) for h in heads]
        dcol = [_bcast_col(d_all, h, DV) for h in heads]
        m_intra, s_hat = [], []
        for h in heads:
            log_d = jnp.where(causal, bcol[h][:, :L] + d_t[h:h + 1, :], NEG)
            mi = jnp.max(log_d, axis=-1, keepdims=True)
            m_intra.append(mi)
            s_hat.append(qk[h] * jnp.exp(log_d - mi))
        nhx = [_dot(s_hat[h], vx[h]) for h in heads]
        yield

        blast = [bcol[h][L - 1:L, :] for h in heads]
        m_w, kw = [], []
        for h in heads:
            log_w = blast[h] + dcol[h]
            mw = jnp.max(log_w, axis=0, keepdims=True)
            m_w.append(mw)
            kw.append(k[h] * jnp.exp(log_w - mw)[:, :DK])
        ckvx = [_dot(kw[h], vx[h], TN) for h in heads]
        yield

        m = [m_scr[b, h:h + 1, :] for h in heads]
        hcell = []
        for h in heads:
            log_inter = bcol[h] + m[h]
            m_t = jnp.maximum(log_inter, m_intra[h])
            s_inter = jnp.exp(log_inter - m_t)
            s_intra = jnp.exp(m_intra[h] - m_t)
            num = s_inter * qcx[h][:, :DV] + s_intra * nhx[h][:, :DV]
            qn = s_inter * qcx[h][:, DV:] + s_intra * nhx[h][:, DV:]
            hcell.append(num / jnp.maximum(jnp.abs(qn), jnp.exp(-m_t)))
        ms = [_dot(hcell[h] * hcell[h], mean_mat) for h in heads]
        yield

        for h in heads:
            m_new = jnp.maximum(blast[h] + m[h], m_w[h])
            a_old = jnp.exp(blast[h] + m[h] - m_new)
            a_new = jnp.exp(m_w[h] - m_new)
            cx_scr[b, h] = (jnp.concatenate([a_old, a_old], axis=1) * cx[h]
                            + jnp.concatenate([a_new, a_new], axis=1) * ckvx[h])
            m_scr[b, h:h + 1, :] = m_new
        yield

        for h in heads:
            oh = op_ref[b, :, h * DV:(h + 1) * DV]
            o_ref[b, :, h * DV:(h + 1) * DV] = (
                hcell[h] * lax.rsqrt(ms[h] + RMS_EPS) * gn_ref[:, h * DV:(h + 1) * DV]
                * _sigmoid(oh)).astype(o_ref.dtype)

    _run_lockstep(sequence_stages(b) for b in range(BB))

    @pl.when(t == pl.num_programs(1) - 1)
    def _():
        cxout_ref[...] = cx_scr[...]
        mout_ref[...] = m_scr[...]


def _mlstm_recurrence(q, k, v, o_pre, gates, cx0, m0x, bias_row, gnorm_row, B, T, L):
    nt = T // L
    H, DK, DV = ML_HEADS, ML_DK, ML_DV
    BB = _seqs_per_step(B, ML_SEQS_PER_STEP)
    tok = lambda b, t: (b, t, 0)
    per_seq3 = lambda b, t: (b, 0, 0)
    per_seq4 = lambda b, t: (b, 0, 0, 0)
    const2 = lambda b, t: (0, 0)
    return pl.pallas_call(
        functools.partial(_mlstm_kernel, L=L, BB=BB),
        grid=(B // BB, nt),
        in_specs=[
            pl.BlockSpec((BB, L, ML_KW), tok),
            pl.BlockSpec((BB, L, ML_KW), tok),
            pl.BlockSpec((BB, L, ML_VW), tok),
            pl.BlockSpec((BB, L, ML_VW), tok),
            pl.BlockSpec((BB, L, LANES), tok),
            pl.BlockSpec((BB, H, DK, 2 * DV), per_seq4),
            pl.BlockSpec((BB, H, LANES), per_seq3),
            pl.BlockSpec((1, LANES), const2),
            pl.BlockSpec((1, ML_VW), const2),
        ],
        out_specs=[
            pl.BlockSpec((BB, L, ML_VW), tok),
            pl.BlockSpec((BB, H, DK, 2 * DV), per_seq4),
            pl.BlockSpec((BB, H, LANES), per_seq3),
        ],
        out_shape=[
            jax.ShapeDtypeStruct((B, T, ML_VW), BF16),
            jax.ShapeDtypeStruct((B, H, DK, 2 * DV), F32),
            jax.ShapeDtypeStruct((B, H, LANES), F32),
        ],
        scratch_shapes=[
            pltpu.VMEM((BB, H, DK, 2 * DV), F32),
            pltpu.VMEM((BB, H, LANES), F32),
        ],
        compiler_params=pltpu.CompilerParams(
            dimension_semantics=("arbitrary", "arbitrary"), vmem_limit_bytes=VMEM_LIMIT),
        name="mlstm_recurrence",
    )(q, k, v, o_pre, gates, cx0, m0x, bias_row, gnorm_row)


def _out_ffn_kernel(x_ref, o_ref, wo_ref, g_ref, wgu_ref, wd_ref, gfin_ref, y_ref, *, final):
    x1 = x_ref[...] + jnp.dot(o_ref[...], wo_ref[...], preferred_element_type=F32)
    hn = _rms(x1, g_ref[...]).astype(BF16)
    acc = x1
    for c in range(FFN_HIDDEN // FFN_CHUNK):
        lo = c * FFN_CHUNK
        gate = jnp.dot(hn, wgu_ref[:, lo:lo + FFN_CHUNK], preferred_element_type=F32)
        up = jnp.dot(hn, wgu_ref[:, FFN_HIDDEN + lo:FFN_HIDDEN + lo + FFN_CHUNK],
                     preferred_element_type=F32)
        act = (gate * _sigmoid(gate) * up).astype(BF16)
        acc = acc + jnp.dot(act, wd_ref[lo:lo + FFN_CHUNK, :], preferred_element_type=F32)
    if final:
        acc = _rms(acc, gfin_ref[...])
    y_ref[...] = acc


def _out_ffn(x, o, w_out, g, w_gu, w_down, layer, g_final, final):
    n, d = x.shape
    tm = min(TOKEN_TILE, n)
    assert n % tm == 0 and FFN_HIDDEN % FFN_CHUNK == 0 and o.dtype == BF16
    tile = lambda i: (i, 0)
    const = lambda i: (0, 0)
    this_layer = lambda i: (layer, 0, 0)
    resident = pl.Buffered(1)
    return pl.pallas_call(
        functools.partial(_out_ffn_kernel, final=final),
        grid=(n // tm,),
        in_specs=[
            pl.BlockSpec((tm, d), tile),
            pl.BlockSpec((tm, o.shape[1]), tile),
            pl.BlockSpec(w_out.shape, const, pipeline_mode=resident),
            pl.BlockSpec((1, d), const),
            pl.BlockSpec((None,) + w_gu.shape[1:], this_layer, pipeline_mode=resident),
            pl.BlockSpec((None,) + w_down.shape[1:], this_layer, pipeline_mode=resident),
            pl.BlockSpec((1, d), const),
        ],
        out_specs=pl.BlockSpec((tm, d), tile),
        out_shape=jax.ShapeDtypeStruct((n, d), F32),
        compiler_params=pltpu.CompilerParams(
            dimension_semantics=("arbitrary",), vmem_limit_bytes=VMEM_LIMIT),
        name="out_ffn",
    )(x, o, w_out, g, w_gu, w_down, g_final)


def _pad_lanes(row, offset=0):
    out = jnp.zeros((1, LANES), F32)
    return lax.dynamic_update_slice(out, row.astype(F32)[None, :], (0, offset))


def _prep_weights(norm_mix, gdn_w_in, gdn_conv_w, gdn_a_log, gdn_dt_bias, gdn_norm, gdn_w_out,
                  ml_w_in, ml_b_i, ml_b_f, ml_norm, ml_w_out, norm_ffn, ffn_w_gu, ffn_w_down,
                  norm_final):
    H = GDN_HEADS
    w0 = gdn_w_in[0]
    gate_pad = jnp.zeros((D_MODEL, LANES - 2 * H), F32)
    w1 = ml_w_in[0]
    p = {
        "g_mix0": norm_mix[0][None, :], "g_mix1": norm_mix[1][None, :],
        "g_ffn0": norm_ffn[0][None, :], "g_ffn1": norm_ffn[1][None, :],
        "g_final": norm_final[None, :],
        "gdn_w_in": w0.astype(BF16),
        "gdn_w_gates": jnp.concatenate([w0[:, GDN_QKV + GDN_VW:], gate_pad], axis=1).astype(BF16),
        "gdn_conv_w": gdn_conv_w[0],
        "gdn_alog": _pad_lanes(gdn_a_log[0]), "gdn_dtb": _pad_lanes(gdn_dt_bias[0]),
        "gdn_norm": gdn_norm[0][None, :],
        "gdn_w_out": gdn_w_out[0].astype(BF16),
        "ml_w_in": w1.astype(BF16),
        "ml_w_gates": jnp.concatenate([w1[:, 2 * ML_KW + 2 * ML_VW:], gate_pad], axis=1).astype(BF16),
        "ml_bias": _pad_lanes(jnp.concatenate([ml_b_i[0], ml_b_f[0]])),
        "ml_norm": ml_norm[0][None, :],
        "ml_w_out": ml_w_out[0].astype(BF16),
        "ffn_w_gu": ffn_w_gu.astype(BF16), "ffn_w_down": ffn_w_down.astype(BF16),
    }
    return p


def _trunk(x, conv0, s0, c0, n0, m0, L, p):
    B, T, D = x.shape
    N = B * T
    x2 = x.reshape(N, D)
    seq = lambda a: a.reshape(B, T, a.shape[-1])

    prev8 = jnp.concatenate(
        [jnp.zeros((B, SUBLANES - (CONV_W - 1), GDN_QKV), F32), conv0.astype(F32)], axis=1)
    conv_in_proj = T % TOKEN_TILE == 0
    if conv_in_proj:
        qkv, z, gates, raw_tail = _norm_proj_conv(x2, p["g_mix0"], p["gdn_w_in"], p["gdn_w_gates"],
                                                  prev8, p["gdn_conv_w"], T)
        conv_fin = raw_tail.reshape(B, T // TOKEN_TILE, SUBLANES, GDN_QKV)[
            :, -1, SUBLANES - (CONV_W - 1):, :]
    else:
        qkv, z, gates = _norm_proj(x2, p["g_mix0"], [
            (p["gdn_w_in"], GDN_QKV, 0), (p["gdn_w_in"], GDN_VW, GDN_QKV // GDN_VW),
            (p["gdn_w_gates"], LANES, 0)])
        conv_fin = seq(qkv)[:, T - (CONV_W - 1):, :]
    o, s_fin = _gdn_recurrence(seq(qkv), seq(z), seq(gates), prev8, s0, p["gdn_conv_w"],
                               p["gdn_alog"], p["gdn_dtb"], p["gdn_norm"], B, T, L, conv_in_proj)
    x2 = _out_ffn(x2, o.reshape(N, GDN_VW), p["gdn_w_out"], p["g_ffn0"], p["ffn_w_gu"],
                  p["ffn_w_down"], 0, p["g_final"], final=False)

    q, k, v, o_pre, gates = _norm_proj(x2, p["g_mix1"], [
        (p["ml_w_in"], ML_KW, 0), (p["ml_w_in"], ML_KW, 1),
        (p["ml_w_in"], ML_VW, 2 * ML_KW // ML_VW), (p["ml_w_in"], ML_VW, 2 * ML_KW // ML_VW + 1),
        (p["ml_w_gates"], LANES, 0)])
    m0x = jnp.broadcast_to(m0[:, :, None], (B, ML_HEADS, LANES))
    cx0 = jnp.concatenate(
        [c0, jnp.broadcast_to(n0[:, :, :, None], (B, ML_HEADS, ML_DK, ML_DV))], axis=-1)
    o, cx_fin, m_fin = _mlstm_recurrence(seq(q), seq(k), seq(v), seq(o_pre), seq(gates), cx0, m0x,
                                         p["ml_bias"], p["ml_norm"], B, T, L)
    c_fin, n_fin = cx_fin[..., :ML_DV], cx_fin[..., ML_DV]
    y = _out_ffn(x2, o.reshape(N, ML_VW), p["ml_w_out"], p["g_ffn1"], p["ffn_w_gu"],
                 p["ffn_w_down"], 1, p["g_final"], final=True)
    return (y.reshape(B, T, D), conv_fin[None], s_fin[None], c_fin[None], n_fin[None],
            m_fin[:, :, 0][None])


def kernel(x_prompt, x_sample, state_gdn_conv, state_gdn_S, state_mlstm_C, state_mlstm_n,
           state_mlstm_m, norm_mix, gdn_w_in, gdn_conv_w, gdn_a_log, gdn_dt_bias, gdn_norm,
           gdn_w_out, ml_w_in, ml_b_i, ml_b_f, ml_norm, ml_w_out, norm_ffn, ffn_w_gu,
           ffn_w_down, norm_final):
    p = _prep_weights(norm_mix, gdn_w_in, gdn_conv_w, gdn_a_log, gdn_dt_bias, gdn_norm, gdn_w_out,
                      ml_w_in, ml_b_i, ml_b_f, ml_norm, ml_w_out, norm_ffn, ffn_w_gu, ffn_w_down,
                      norm_final)
    Bp = x_prompt.shape[0]
    prompt = _trunk(
        x_prompt,
        jnp.zeros((Bp, CONV_W - 1, GDN_QKV), F32),
        jnp.zeros((Bp, GDN_HEADS, GDN_DK, GDN_DV), F32),
        jnp.zeros((Bp, ML_HEADS, ML_DK, ML_DV), F32),
        jnp.zeros((Bp, ML_HEADS, ML_DK), F32),
        jnp.zeros((Bp, ML_HEADS), F32),
        PROMPT_CHUNK, p)
    sample = _trunk(
        x_sample, state_gdn_conv[0], state_gdn_S[0], state_mlstm_C[0], state_mlstm_n[0],
        state_mlstm_m[0], x_sample.shape[1], p)
    return (prompt[0], sample[0]) + prompt[1:] + sample[1:]
```

```python
import functools

import jax
import jax.numpy as jnp
from jax import lax
from jax.experimental import pallas as pl
from jax.experimental.pallas import tpu as pltpu

F32 = jnp.float32
BF16 = jnp.bfloat16

D_MODEL = 1024
PROMPT_CHUNK = 64
GDN_HEADS = 8
GDN_DK = 128
GDN_DV = 128
GDN_KW = GDN_HEADS * GDN_DK
GDN_VW = GDN_HEADS * GDN_DV
GDN_QKV = 2 * GDN_KW + GDN_VW
CONV_W = 4
ML_HEADS = 8
ML_DK = 64
ML_DV = 128
ML_KW = ML_HEADS * ML_DK
ML_VW = ML_HEADS * ML_DV
FFN_HIDDEN = 2816
RMS_EPS = 1e-6
L2_EPS = 1e-6
NEG = -1e30

LANES = 128
SUBLANES = 8
TOKEN_TILE = 1024
FFN_CHUNK = 256
CONV_COLS = 512
GDN_SEQS_PER_STEP = 4
ML_SEQS_PER_STEP = 4
VMEM_LIMIT = 56 * 1024 * 1024

NN = ((1,), (0,))
NT = ((1,), (1,))
TN = ((0,), (0,))


def _dot(a, b, dims=NN):
    return lax.dot_general(a.astype(BF16), b.astype(BF16), (dims, ((), ())),
                           preferred_element_type=F32)


def _dot_sel(sel, x, dims=NN):
    hi = x.astype(BF16)
    r1 = x - hi.astype(F32)
    mid = r1.astype(BF16)
    lo = (r1 - mid.astype(F32)).astype(BF16)
    d = lambda p: lax.dot_general(sel, p, (dims, ((), ())), preferred_element_type=F32)
    return d(hi) + d(mid) + d(lo)


def _sigmoid(x):
    return 1.0 / (1.0 + jnp.exp(-x))


def _softplus(x):
    return jnp.maximum(x, 0.0) + jnp.log1p(jnp.exp(-jnp.abs(x)))


def _rms(x, g):
    return x * lax.rsqrt(jnp.mean(x * x, axis=-1, keepdims=True) + RMS_EPS) * g


def _masks(L):
    ri = lax.broadcasted_iota(jnp.int32, (L, L), 0)
    ci = lax.broadcasted_iota(jnp.int32, (L, L), 1)
    return ri >= ci, ri > ci


def _bcast_col(x, c, width):
    return jnp.broadcast_to(x[:, c:c + 1], (x.shape[0], width))


def _causal_conv(x, tail, cw):
    tail_row = lax.broadcasted_iota(jnp.int32, tail.shape, 0)
    y = x * cw[CONV_W - 1:CONV_W]
    for s in range(1, CONV_W):
        xs = pltpu.roll(x, s, 0)
        head = jnp.where(tail_row < s, pltpu.roll(tail, s, 0), xs[:SUBLANES])
        xs = jnp.concatenate([head, xs[SUBLANES:]], axis=0)
        y = y + xs * cw[CONV_W - 1 - s:CONV_W - s]
    return y


def _norm_proj_kernel(x_ref, g_ref, *refs, n_out):
    w_refs, o_refs = refs[:n_out], refs[n_out:]
    h = _rms(x_ref[...], g_ref[...]).astype(BF16)
    for w_ref, o_ref in zip(w_refs, o_refs):
        o_ref[...] = jnp.dot(h, w_ref[...], preferred_element_type=F32).astype(o_ref.dtype)


def _norm_proj(x, g, weights):
    n, d = x.shape
    tm = min(TOKEN_TILE, n)
    assert n % tm == 0
    in_specs = [pl.BlockSpec((tm, d), lambda i: (i, 0)),
                pl.BlockSpec((1, d), lambda i: (0, 0))]
    in_specs += [pl.BlockSpec((d, cols), functools.partial(lambda i, blk: (0, blk), blk=blk),
                              pipeline_mode=pl.Buffered(1)) for _, cols, blk, _ in weights]
    out_specs = [pl.BlockSpec((tm, cols), lambda i: (i, 0)) for _, cols, _, _ in weights]
    out_shape = [jax.ShapeDtypeStruct((n, cols), dt) for _, cols, _, dt in weights]
    return pl.pallas_call(
        functools.partial(_norm_proj_kernel, n_out=len(weights)),
        grid=(n // tm,),
        in_specs=in_specs, out_specs=out_specs, out_shape=out_shape,
        compiler_params=pltpu.CompilerParams(
            dimension_semantics=("arbitrary",), vmem_limit_bytes=VMEM_LIMIT),
        name="norm_proj",
    )(x, g, *[w for w, _, _, _ in weights])


def _norm_proj_conv_kernel(x_ref, g_ref, wqkv_ref, wz_ref, wg_ref, prev_ref, convw_ref,
                           y_ref, z_ref, gates_ref, rawtail_ref, tail_scr, *, tiles_per_seq):
    i = pl.program_id(0)

    @pl.when(i % tiles_per_seq == 0)
    def _():
        tail_scr[...] = prev_ref[...]

    tm = x_ref.shape[0]
    h = _rms(x_ref[...], g_ref[...]).astype(BF16)
    for j in range(GDN_QKV // CONV_COLS):
        cols = slice(j * CONV_COLS, (j + 1) * CONV_COLS)
        raw = jnp.dot(h, wqkv_ref[:, cols], preferred_element_type=F32)
        y_ref[:, cols] = _causal_conv(raw, tail_scr[:, cols], convw_ref[:, cols])
        tail_scr[:, cols] = raw[tm - SUBLANES:tm]
        rawtail_ref[:, cols] = raw[tm - SUBLANES:tm]
    z_ref[...] = jnp.dot(h, wz_ref[...], preferred_element_type=F32)
    gates_ref[...] = jnp.dot(h, wg_ref[...], preferred_element_type=F32)


def _norm_proj_conv(x, g, w_in, w_gates, prev8, conv_w, T):
    n, d = x.shape
    tm = TOKEN_TILE
    assert T % tm == 0 and GDN_QKV % CONV_COLS == 0 and GDN_QKV % GDN_VW == 0
    tiles_per_seq = T // tm
    tile = lambda i: (i, 0)
    const = lambda i: (0, 0)
    resident = pl.Buffered(1)
    return pl.pallas_call(
        functools.partial(_norm_proj_conv_kernel, tiles_per_seq=tiles_per_seq),
        grid=(n // tm,),
        in_specs=[
            pl.BlockSpec((tm, d), tile),
            pl.BlockSpec((1, d), const),
            pl.BlockSpec((d, GDN_QKV), const, pipeline_mode=resident),
            pl.BlockSpec((d, GDN_VW), lambda i: (0, GDN_QKV // GDN_VW), pipeline_mode=resident),
            pl.BlockSpec((d, LANES), const, pipeline_mode=resident),
            pl.BlockSpec((None, SUBLANES, GDN_QKV), lambda i: (i // tiles_per_seq, 0, 0)),
            pl.BlockSpec((CONV_W, GDN_QKV), const),
        ],
        out_specs=[
            pl.BlockSpec((tm, GDN_QKV), tile),
            pl.BlockSpec((tm, GDN_VW), tile),
            pl.BlockSpec((tm, LANES), tile),
            pl.BlockSpec((None, SUBLANES, GDN_QKV), lambda i: (i, 0, 0)),
        ],
        out_shape=[
            jax.ShapeDtypeStruct((n, GDN_QKV), F32),
            jax.ShapeDtypeStruct((n, GDN_VW), F32),
            jax.ShapeDtypeStruct((n, LANES), F32),
            jax.ShapeDtypeStruct((n // tm, SUBLANES, GDN_QKV), F32),
        ],
        scratch_shapes=[pltpu.VMEM((SUBLANES, GDN_QKV), F32)],
        compiler_params=pltpu.CompilerParams(
            dimension_semantics=("arbitrary",), vmem_limit_bytes=VMEM_LIMIT),
        name="norm_proj_conv",
    )(x, g, w_in, w_in, w_gates, prev8, conv_w)


def _run_lockstep(stage_generators):
    active = list(stage_generators)
    while active:
        for g in list(active):
            try:
                next(g)
            except StopIteration:
                active.remove(g)


def _unit_lower_inverses(a_list, L):
    ri = lax.broadcasted_iota(jnp.int32, (L, L), 0)
    ci = lax.broadcasted_iota(jnp.int32, (L, L), 1)
    eye = jnp.where(ri == ci, 1.0, 0.0)
    ms = [-a for a in a_list]
    ps = [eye + m for m in ms]
    mb = [m.astype(BF16) for m in ms]
    ms = [_dot(m, m) for m in mb]
    span = 2
    while span < L:
        yield
        mb = [m.astype(BF16) for m in ms]
        pb = [p.astype(BF16) for p in ps]
        if 2 * span >= L:
            ps = [p + _dot(b, m) for p, b, m in zip(ps, pb, mb)]
        else:
            rs = [_dot(jnp.concatenate([b, m], axis=0), m) for b, m in zip(pb, mb)]
            ps = [p + r[:L] for p, r in zip(ps, rs)]
            ms = [r[L:] for r in rs]
        span *= 2
    return ps


def _gdn_kernel(qkv_ref, z_ref, gates_ref, prev_ref, s0_ref, convw_ref, alog_ref, dtb_ref, gn_ref,
                o_ref, sout_ref, tail_scr, s_scr, *, L, BB, conv_done):
    H, DK, DV, KW = GDN_HEADS, GDN_DK, GDN_DV, GDN_KW
    t = pl.program_id(1)

    @pl.when(t == 0)
    def _():
        tail_scr[...] = prev_ref[...]
        s_scr[...] = s0_ref[...]

    causal, strict = _masks(L)
    causal_f = jnp.where(causal, 1.0, 0.0)
    strict_f = jnp.where(strict, 1.0, 0.0)
    tril = causal_f.astype(BF16)
    cw = convw_ref[...]
    gn = gn_ref[...]
    sum_mat = jnp.ones((DK, DK), BF16)
    mean_mat = jnp.full((DV, DV), 1.0 / DV, BF16)
    heads = range(H)

    def sequence_stages(b):
        x = qkv_ref[b]
        if not conv_done:
            conv = _causal_conv(x, tail_scr[b], cw)
            tail_scr[b] = x[L - SUBLANES:L]
            x = conv
        y = x * _sigmoid(x)
        gt = gates_ref[b]
        g_all = -jnp.exp(alog_ref[...]) * _softplus(gt + dtb_ref[...])
        beta_all = _sigmoid(gt)
        dec = _dot_sel(tril, g_all)
        dec_t = dec.T
        qr = [y[:, h * DK:(h + 1) * DK] for h in heads]
        kr = [y[:, KW + h * DK:KW + (h + 1) * DK] for h in heads]
        v = [y[:, 2 * KW + h * DV:2 * KW + (h + 1) * DV].astype(BF16) for h in heads]
        ss = [_dot(jnp.concatenate([qr[h] * qr[h], kr[h] * kr[h]], axis=0), sum_mat) for h in heads]
        yield

        q = [(qr[h] * lax.rsqrt(ss[h][:L] + L2_EPS) * (DK ** -0.5)).astype(BF16) for h in heads]
        k = [(kr[h] * lax.rsqrt(ss[h][L:] + L2_EPS)).astype(BF16) for h in heads]
        qk = [_dot(jnp.concatenate([q[h], k[h]], axis=0), k[h], NT) for h in heads]
        dcol = [_bcast_col(dec, h, DK) for h in heads]
        bcol = [_bcast_col(beta_all, H + h, DK) for h in heads]
        gamma = []
        for h in heads:
            diff = dcol[h][:, :L] - dec_t[h:h + 1, :]
            gamma.append(jnp.exp(jnp.minimum(diff, 0.0)) * causal_f)
        yield

        attn = [qk[h][:L] * gamma[h] for h in heads]
        a = [bcol[h][:, :L] * qk[h][L:] * (gamma[h] * strict_f) for h in heads]
        tm = yield from _unit_lower_inverses(a, L)
        edec = [jnp.exp(dcol[h]) for h in heads]
        uw = [_dot(tm[h], jnp.concatenate([v[h] * bcol[h].astype(BF16),
                                           k[h] * (bcol[h] * edec[h]).astype(BF16)], axis=1))
              for h in heads]
        dlast = [dcol[h][L - 1:L, :] for h in heads]
        k_tail = [k[h] * jnp.exp(dlast[h] - dcol[h]).astype(BF16) for h in heads]
        q_dec = [q[h] * edec[h].astype(BF16) for h in heads]
        yield

        s = [s_scr[b, h] for h in heads]
        wq = [_dot(jnp.concatenate([uw[h][:, DV:].astype(BF16), q_dec[h]], axis=0), s[h]) for h in heads]
        yield

        v_new = [uw[h][:, :DV] - wq[h][:L] for h in heads]
        o_intra = [_dot(attn[h], v_new[h]) for h in heads]
        ds = [_dot(k_tail[h], v_new[h], TN) for h in heads]
        yield

        for h in heads:
            s_scr[b, h] = s[h] * jnp.exp(dlast[h]) + ds[h]
        o = [wq[h][L:] + o_intra[h] for h in heads]
        ms = [_dot(o[h] * o[h], mean_mat) for h in heads]
        yield

        for h in heads:
            zh = z_ref[b, :, h * DV:(h + 1) * DV]
            o_ref[b, :, h * DV:(h + 1) * DV] = (
                o[h] * lax.rsqrt(ms[h] + RMS_EPS) * gn * (zh * _sigmoid(zh))).astype(o_ref.dtype)

    _run_lockstep(sequence_stages(b) for b in range(BB))

    @pl.when(t == pl.num_programs(1) - 1)
    def _():
        sout_ref[...] = s_scr[...]


def _seqs_per_step(B, want):
    bb = min(B, want)
    assert B % bb == 0
    return bb


def _gdn_recurrence(qkv, z, gates, prev8, s0, conv_w, alog_row, dtb_row, gnorm_row, B, T, L, conv_done):
    nt = T // L
    H, DK, DV = GDN_HEADS, GDN_DK, GDN_DV
    BB = _seqs_per_step(B, GDN_SEQS_PER_STEP)
    tok = lambda b, t: (b, t, 0)
    per_seq3 = lambda b, t: (b, 0, 0)
    per_seq4 = lambda b, t: (b, 0, 0, 0)
    const2 = lambda b, t: (0, 0)
    return pl.pallas_call(
        functools.partial(_gdn_kernel, L=L, BB=BB, conv_done=conv_done),
        grid=(B // BB, nt),
        in_specs=[
            pl.BlockSpec((BB, L, GDN_QKV), tok),
            pl.BlockSpec((BB, L, GDN_VW), tok),
            pl.BlockSpec((BB, L, LANES), tok),
            pl.BlockSpec((BB, SUBLANES, GDN_QKV), per_seq3),
            pl.BlockSpec((BB, H, DK, DV), per_seq4),
            pl.BlockSpec((CONV_W, GDN_QKV), const2),
            pl.BlockSpec((1, LANES), const2),
            pl.BlockSpec((1, LANES), const2),
            pl.BlockSpec((1, DV), const2),
        ],
        out_specs=[
            pl.BlockSpec((BB, L, GDN_VW), tok),
            pl.BlockSpec((BB, H, DK, DV), per_seq4),
        ],
        out_shape=[
            jax.ShapeDtypeStruct((B, T, GDN_VW), BF16),
            jax.ShapeDtypeStruct((B, H, DK, DV), F32),
        ],
        scratch_shapes=[
            pltpu.VMEM((BB, SUBLANES, GDN_QKV), F32),
            pltpu.VMEM((BB, H, DK, DV), F32),
        ],
        compiler_params=pltpu.CompilerParams(
            dimension_semantics=("arbitrary", "arbitrary"), vmem_limit_bytes=VMEM_LIMIT),
        name="gdn_recurrence",
    )(qkv, z, gates, prev8, s0, conv_w, alog_row, dtb_row, gnorm_row)


def _mlstm_kernel(q_ref, k_ref, v_ref, op_ref, gates_ref, cx0_ref, m0_ref, bias_ref, gn_ref,
                  o_ref, cxout_ref, mout_ref, cx_scr, m_scr, *, L, BB):
    H, DK, DV = ML_HEADS, ML_DK, ML_DV
    t = pl.program_id(1)

    @pl.when(t == 0)
    def _():
        cx_scr[...] = cx0_ref[...]
        m_scr[...] = m0_ref[...]

    causal, _ = _masks(L)
    tril = jnp.where(causal, 1.0, 0.0).astype(BF16)
    ones_v = jnp.ones((L, DV), BF16)
    mean_mat = jnp.full((DV, DV), 1.0 / DV, BF16)
    heads = range(H)

    def sequence_stages(b):
        pre = gates_ref[b] + bias_ref[...]
        bcum = _dot_sel(tril, -_softplus(-pre))
        d_all = pre - pltpu.roll(bcum, LANES - H, 1)
        d_t = d_all.T
        q = [q_ref[b, :, h * DK:(h + 1) * DK].astype(BF16) for h in heads]
        k = [k_ref[b, :, h * DK:(h + 1) * DK] * (DK ** -0.5) for h in heads]
        vx = [jnp.concatenate([v_ref[b, :, h * DV:(h + 1) * DV].astype(BF16), ones_v], axis=1)
              for h in heads]
        cx = [cx_scr[b, h] for h in heads]
        qk = [_dot(q[h], k[h], NT) for h in heads]
        qcx = [_dot(q[h], cx[h]) for h in heads]
        yield

        bcol = [_bcast_col(bcum, H + h, DV) for h in heads]
        dcol = [_bcast_col(d_all, h, DV) for h in heads]
        m_intra, s_hat = [], []
        for h in heads:
            log_d = jnp.where(causal, bcol[h][:, :L] + d_t[h:h + 1, :], NEG)
            mi = jnp.max(log_d, axis=-1, keepdims=True)
            m_intra.append(mi)
            s_hat.append(qk[h] * jnp.exp(log_d - mi))
        nhx = [_dot(s_hat[h], vx[h]) for h in heads]
        yield

        blast = [bcol[h][L - 1:L, :] for h in heads]
        m_w, kw = [], []
        for h in heads:
            log_w = blast[h] + dcol[h]
            mw = jnp.max(log_w, axis=0, keepdims=True)
            m_w.append(mw)
            kw.append(k[h] * jnp.exp(log_w - mw)[:, :DK])
        ckvx = [_dot(kw[h], vx[h], TN) for h in heads]
        yield

        m = [m_scr[b, h:h + 1, :] for h in heads]
        hcell = []
        for h in heads:
            log_inter = bcol[h] + m[h]
            m_t = jnp.maximum(log_inter, m_intra[h])
            s_inter = jnp.exp(log_inter - m_t)
            s_intra = jnp.exp(m_intra[h] - m_t)
            num = s_inter * qcx[h][:, :DV] + s_intra * nhx[h][:, :DV]
            qn = s_inter * qcx[h][:, DV:] + s_intra * nhx[h][:, DV:]
            hcell.append(num / jnp.maximum(jnp.abs(qn), jnp.exp(-m_t)))
        ms = [_dot(hcell[h] * hcell[h], mean_mat) for h in heads]
        yield

        for h in heads:
            m_new = jnp.maximum(blast[h] + m[h], m_w[h])
            a_old = jnp.exp(blast[h] + m[h] - m_new)
            a_new = jnp.exp(m_w[h] - m_new)
            cx_scr[b, h] = (jnp.concatenate([a_old, a_old], axis=1) * cx[h]
                            + jnp.concatenate([a_new, a_new], axis=1) * ckvx[h])
            m_scr[b, h:h + 1, :] = m_new
        yield

        for h in heads:
            oh = op_ref[b, :, h * DV:(h + 1) * DV]
            o_ref[b, :, h * DV:(h + 1) * DV] = (
                hcell[h] * lax.rsqrt(ms[h] + RMS_EPS) * gn_ref[:, h * DV:(h + 1) * DV]
                * _sigmoid(oh)).astype(o_ref.dtype)

    _run_lockstep(sequence_stages(b) for b in range(BB))

    @pl.when(t == pl.num_programs(1) - 1)
    def _():
        cxout_ref[...] = cx_scr[...]
        mout_ref[...] = m_scr[...]


def _mlstm_recurrence(q, k, v, o_pre, gates, cx0, m0x, bias_row, gnorm_row, B, T, L):
    nt = T // L
    H, DK, DV = ML_HEADS, ML_DK, ML_DV
    BB = _seqs_per_step(B, ML_SEQS_PER_STEP)
    tok = lambda b, t: (b, t, 0)
    per_seq3 = lambda b, t: (b, 0, 0)
    per_seq4 = lambda b, t: (b, 0, 0, 0)
    const2 = lambda b, t: (0, 0)
    return pl.pallas_call(
        functools.partial(_mlstm_kernel, L=L, BB=BB),
        grid=(B // BB, nt),
        in_specs=[
            pl.BlockSpec((BB, L, ML_KW), tok),
            pl.BlockSpec((BB, L, ML_KW), tok),
            pl.BlockSpec((BB, L, ML_VW), tok),
            pl.BlockSpec((BB, L, ML_VW), tok),
            pl.BlockSpec((BB, L, LANES), tok),
            pl.BlockSpec((BB, H, DK, 2 * DV), per_seq4),
            pl.BlockSpec((BB, H, LANES), per_seq3),
            pl.BlockSpec((1, LANES), const2),
            pl.BlockSpec((1, ML_VW), const2),
        ],
        out_specs=[
            pl.BlockSpec((BB, L, ML_VW), tok),
            pl.BlockSpec((BB, H, DK, 2 * DV), per_seq4),
            pl.BlockSpec((BB, H, LANES), per_seq3),
        ],
        out_shape=[
            jax.ShapeDtypeStruct((B, T, ML_VW), BF16),
            jax.ShapeDtypeStruct((B, H, DK, 2 * DV), F32),
            jax.ShapeDtypeStruct((B, H, LANES), F32),
        ],
        scratch_shapes=[
            pltpu.VMEM((BB, H, DK, 2 * DV), F32),
            pltpu.VMEM((BB, H, LANES), F32),
        ],
        compiler_params=pltpu.CompilerParams(
            dimension_semantics=("arbitrary", "arbitrary"), vmem_limit_bytes=VMEM_LIMIT),
        name="mlstm_recurrence",
    )(q, k, v, o_pre, gates, cx0, m0x, bias_row, gnorm_row)


def _out_ffn_kernel(x_ref, o_ref, wo_ref, g_ref, wgu_ref, wd_ref, gfin_ref, y_ref, *, final):
    x1 = x_ref[...] + jnp.dot(o_ref[...], wo_ref[...], preferred_element_type=F32)
    hn = _rms(x1, g_ref[...]).astype(BF16)
    acc = x1
    for c in range(FFN_HIDDEN // FFN_CHUNK):
        lo = c * FFN_CHUNK
        gate = jnp.dot(hn, wgu_ref[:, lo:lo + FFN_CHUNK], preferred_element_type=F32)
        up = jnp.dot(hn, wgu_ref[:, FFN_HIDDEN + lo:FFN_HIDDEN + lo + FFN_CHUNK],
                     preferred_element_type=F32)
        act = (gate * _sigmoid(gate) * up).astype(BF16)
        acc = acc + jnp.dot(act, wd_ref[lo:lo + FFN_CHUNK, :], preferred_element_type=F32)
    if final:
        acc = _rms(acc, gfin_ref[...])
    y_ref[...] = acc


def _out_ffn(x, o, w_out, g, w_gu, w_down, layer, g_final, final):
    n, d = x.shape
    tm = min(TOKEN_TILE, n)
    assert n % tm == 0 and FFN_HIDDEN % FFN_CHUNK == 0 and o.dtype == BF16
    tile = lambda i: (i, 0)
    const = lambda i: (0, 0)
    this_layer = lambda i: (layer, 0, 0)
    resident = pl.Buffered(1)
    return pl.pallas_call(
        functools.partial(_out_ffn_kernel, final=final),
        grid=(n // tm,),
        in_specs=[
            pl.BlockSpec((tm, d), tile),
            pl.BlockSpec((tm, o.shape[1]), tile),
            pl.BlockSpec(w_out.shape, const, pipeline_mode=resident),
            pl.BlockSpec((1, d), const),
            pl.BlockSpec((None,) + w_gu.shape[1:], this_layer, pipeline_mode=resident),
            pl.BlockSpec((None,) + w_down.shape[1:], this_layer, pipeline_mode=resident),
            pl.BlockSpec((1, d), const),
        ],
        out_specs=pl.BlockSpec((tm, d), tile),
        out_shape=jax.ShapeDtypeStruct((n, d), F32),
        compiler_params=pltpu.CompilerParams(
            dimension_semantics=("arbitrary",), vmem_limit_bytes=VMEM_LIMIT),
        name="out_ffn",
    )(x, o, w_out, g, w_gu, w_down, g_final)


def _pad_lanes(row, offset=0):
    out = jnp.zeros((1, LANES), F32)
    return lax.dynamic_update_slice(out, row.astype(F32)[None, :], (0, offset))


def _prep_weights(norm_mix, gdn_w_in, gdn_conv_w, gdn_a_log, gdn_dt_bias, gdn_norm, gdn_w_out,
                  ml_w_in, ml_b_i, ml_b_f, ml_norm, ml_w_out, norm_ffn, ffn_w_gu, ffn_w_down,
                  norm_final):
    H = GDN_HEADS
    w0 = gdn_w_in[0]
    gate_pad = jnp.zeros((D_MODEL, LANES - 2 * H), F32)
    w1 = ml_w_in[0]
    p = {
        "g_mix0": norm_mix[0][None, :], "g_mix1": norm_mix[1][None, :],
        "g_ffn0": norm_ffn[0][None, :], "g_ffn1": norm_ffn[1][None, :],
        "g_final": norm_final[None, :],
        "gdn_w_in": w0.astype(BF16),
        "gdn_w_gates": jnp.concatenate([w0[:, GDN_QKV + GDN_VW:], gate_pad], axis=1).astype(BF16),
        "gdn_conv_w": gdn_conv_w[0],
        "gdn_alog": _pad_lanes(gdn_a_log[0]), "gdn_dtb": _pad_lanes(gdn_dt_bias[0]),
        "gdn_norm": gdn_norm[0][None, :],
        "gdn_w_out": gdn_w_out[0].astype(BF16),
        "ml_w_in": w1.astype(BF16),
        "ml_w_gates": jnp.concatenate([w1[:, 2 * ML_KW + 2 * ML_VW:], gate_pad], axis=1).astype(BF16),
        "ml_bias": _pad_lanes(jnp.concatenate([ml_b_i[0], ml_b_f[0]])),
        "ml_norm": ml_norm[0][None, :],
        "ml_w_out": ml_w_out[0].astype(BF16),
        "ffn_w_gu": ffn_w_gu.astype(BF16), "ffn_w_down": ffn_w_down.astype(BF16),
    }
    return p


def _trunk(x, conv0, s0, c0, n0, m0, L, p):
    B, T, D = x.shape
    N = B * T
    x2 = x.reshape(N, D)
    seq = lambda a: a.reshape(B, T, a.shape[-1])

    prev8 = jnp.concatenate(
        [jnp.zeros((B, SUBLANES - (CONV_W - 1), GDN_QKV), F32), conv0.astype(F32)], axis=1)
    conv_in_proj = T % TOKEN_TILE == 0
    if conv_in_proj:
        qkv, z, gates, raw_tail = _norm_proj_conv(x2, p["g_mix0"], p["gdn_w_in"], p["gdn_w_gates"],
                                                  prev8, p["gdn_conv_w"], T)
        conv_fin = raw_tail.reshape(B, T // TOKEN_TILE, SUBLANES, GDN_QKV)[
            :, -1, SUBLANES - (CONV_W - 1):, :]
    else:
        qkv, z, gates = _norm_proj(x2, p["g_mix0"], [
            (p["gdn_w_in"], GDN_QKV, 0, F32), (p["gdn_w_in"], GDN_VW, GDN_QKV // GDN_VW, F32),
            (p["gdn_w_gates"], LANES, 0, F32)])
        conv_fin = seq(qkv)[:, T - (CONV_W - 1):, :]
    o, s_fin = _gdn_recurrence(seq(qkv), seq(z), seq(gates), prev8, s0, p["gdn_conv_w"],
                               p["gdn_alog"], p["gdn_dtb"], p["gdn_norm"], B, T, L, conv_in_proj)
    x2 = _out_ffn(x2, o.reshape(N, GDN_VW), p["gdn_w_out"], p["g_ffn0"], p["ffn_w_gu"],
                  p["ffn_w_down"], 0, p["g_final"], final=False)

    q, k, v, o_pre, gates = _norm_proj(x2, p["g_mix1"], [
        (p["ml_w_in"], ML_KW, 0, BF16), (p["ml_w_in"], ML_KW, 1, BF16),
        (p["ml_w_in"], ML_VW, 2 * ML_KW // ML_VW, BF16),
        (p["ml_w_in"], ML_VW, 2 * ML_KW // ML_VW + 1, F32),
        (p["ml_w_gates"], LANES, 0, F32)])
    m0x = jnp.broadcast_to(m0[:, :, None], (B, ML_HEADS, LANES))
    cx0 = jnp.concatenate(
        [c0, jnp.broadcast_to(n0[:, :, :, None], (B, ML_HEADS, ML_DK, ML_DV))], axis=-1)
    o, cx_fin, m_fin = _mlstm_recurrence(seq(q), seq(k), seq(v), seq(o_pre), seq(gates), cx0, m0x,
                                         p["ml_bias"], p["ml_norm"], B, T, L)
    c_fin, n_fin = cx_fin[..., :ML_DV], cx_fin[..., ML_DV]
    y = _out_ffn(x2, o.reshape(N, ML_VW), p["ml_w_out"], p["g_ffn1"], p["ffn_w_gu"],
                 p["ffn_w_down"], 1, p["g_final"], final=True)
    return (y.reshape(B, T, D), conv_fin[None], s_fin[None], c_fin[None], n_fin[None],
            m_fin[:, :, 0][None])


def kernel(x_prompt, x_sample, state_gdn_conv, state_gdn_S, state_mlstm_C, state_mlstm_n,
           state_mlstm_m, norm_mix, gdn_w_in, gdn_conv_w, gdn_a_log, gdn_dt_bias, gdn_norm,
           gdn_w_out, ml_w_in, ml_b_i, ml_b_f, ml_norm, ml_w_out, norm_ffn, ffn_w_gu,
           ffn_w_down, norm_final):
    p = _prep_weights(norm_mix, gdn_w_in, gdn_conv_w, gdn_a_log, gdn_dt_bias, gdn_norm, gdn_w_out,
                      ml_w_in, ml_b_i, ml_b_f, ml_norm, ml_w_out, norm_ffn, ffn_w_gu, ffn_w_down,
                      norm_final)
    Bp = x_prompt.shape[0]
    prompt = _trunk(
        x_prompt,
        jnp.zeros((Bp, CONV_W - 1, GDN_QKV), F32),
        jnp.zeros((Bp, GDN_HEADS, GDN_DK, GDN_DV), F32),
        jnp.zeros((Bp, ML_HEADS, ML_DK, ML_DV), F32),
        jnp.zeros((Bp, ML_HEADS, ML_DK), F32),
        jnp.zeros((Bp, ML_HEADS), F32),
        PROMPT_CHUNK, p)
    sample = _trunk(
        x_sample, state_gdn_conv[0], state_gdn_S[0], state_mlstm_C[0], state_mlstm_n[0],
        state_mlstm_m[0], x_sample.shape[1], p)
    return (prompt[0], sample[0]) + prompt[1:] + sample[1:]
```
